```python
import jax, jax.numpy as jnp
from jax import lax
import numpy as np

D_MODEL = 1024
BATCH = 8
SEQ = 4096
DEPTH = 2

POOL_WINDOWS = (2, 4, 8, 16)
POOL_GROUPS = len(POOL_WINDOWS)
POOL_GROUP_DIM = D_MODEL // 8
POOL_WIDTH = POOL_GROUPS * POOL_GROUP_DIM
SGU_CHUNK = 128
SGU_HEADS = 8
SGU_WIDTH = D_MODEL
SGU_HEAD_DIM = SGU_WIDTH // SGU_HEADS
N_BRANCHES = 2
IN_WIDTH = POOL_WIDTH + 2 * SGU_WIDTH + N_BRANCHES * D_MODEL
SPLITS = (POOL_WIDTH,
          POOL_WIDTH + SGU_WIDTH,
          POOL_WIDTH + 2 * SGU_WIDTH,
          POOL_WIDTH + 2 * SGU_WIDTH + D_MODEL)
N_EXPERTS = 16
CAPACITY_FACTOR = 2
D_EXPERT = 2 * D_MODEL
ALPHA = (2 * DEPTH) ** 0.25
BETA = (8 * DEPTH) ** -0.25
LN_EPS = 1e-5

kernel_name = "hybrid_pool_sgu_ec_moe_deepnorm"


def layer_norm(x, g, b):
    xf = x.astype(jnp.float32)
    mu = jnp.mean(xf, axis=-1, keepdims=True)
    xc = xf - mu
    var = jnp.mean(xc * xc, axis=-1, keepdims=True)
    return (xc * lax.rsqrt(var + LN_EPS) * g + b).astype(x.dtype)


def multiscale_pool(a):
    S = a.shape[1]
    af = a.astype(jnp.float32)
    cs = jnp.concatenate([jnp.zeros_like(af[:, :1]), jnp.cumsum(af, axis=1)], axis=1)
    t = jnp.arange(S)
    outs = []
    for g, w in enumerate(POOL_WINDOWS):
        lo = jnp.clip(t - w // 2, 0, S)
        hi = jnp.clip(t + w // 2, 0, S)
        sl = slice(g * POOL_GROUP_DIM, (g + 1) * POOL_GROUP_DIM)
        csg = cs[:, :, sl]
        count = (hi - lo).astype(jnp.float32)[None, :, None]
        outs.append((csg[:, hi] - csg[:, lo]) / count - af[:, :, sl])
    return jnp.concatenate(outs, axis=-1).astype(a.dtype)


def spatial_gating(u, v, ln_g, ln_b, w_s, b_s):
    B, S, _ = v.shape
    v = layer_norm(v, ln_g, ln_b)
    vc = v.reshape(B, S // SGU_CHUNK, SGU_CHUNK, SGU_HEADS, SGU_HEAD_DIM)
    mixed = jnp.einsum('hpq,bcqhd->bcphd', w_s, vc) + b_s.T[None, None, :, :, None]
    return u * mixed.reshape(B, S, SGU_WIDTH)


def hybrid_mixer(h, w_in, b_in, pool_w, pool_scale, sgu_ln_g, sgu_ln_b, sgu_w, sgu_b, p_a, p_b, w_out):
    B, S, _ = h.shape
    proj = h @ w_in + b_in
    a, u, v, ga, gb = jnp.split(proj, SPLITS, axis=-1)
    pa = multiscale_pool(a).reshape(B, S, POOL_GROUPS, POOL_GROUP_DIM)
    ya = jnp.einsum('bsgc,gcd->bsgd', pa, pool_w).reshape(B, S, POOL_WIDTH) * pool_scale
    yb = spatial_gating(jax.nn.gelu(u), jax.nn.gelu(v), sgu_ln_g, sgu_ln_b, sgu_w, sgu_b)
    merged = jax.nn.sigmoid(ga) * (ya @ p_a) + jax.nn.sigmoid(gb) * (yb @ p_b)
    return merged @ w_out


def expert_choice_ffn(h, w_router, w_gate, w_up, w_down):
    B, S, D = h.shape
    cap = CAPACITY_FACTOR * S // N_EXPERTS
    affinity = jax.nn.softmax((h @ w_router).astype(jnp.float32), axis=-1)
    gates, idx = lax.top_k(jnp.swapaxes(affinity, 1, 2), cap)
    xe = jax.vmap(lambda hb, ib: hb[ib])(h, idx)
    hid = jax.nn.silu(jnp.einsum('becd,edf->becf', xe, w_gate)) * jnp.einsum('becd,edf->becf', xe, w_up)
    ye = jnp.einsum('becf,efd->becd', hid, w_down) * gates[..., None].astype(h.dtype)
    return jax.vmap(
        lambda yb, ib: jnp.zeros((S, D), yb.dtype).at[ib.reshape(-1)].add(yb.reshape(-1, D))
    )(ye, idx)


def setup_inputs(seed: int = 0) -> dict:
    key = jax.random.key(seed)
    ks = jax.random.split(key, 24)

    def nrm(k, shape, scale):
        return jax.random.normal(k, shape, jnp.float32) * scale

    L, D = DEPTH, D_MODEL
    return {
        "x": nrm(ks[0], (BATCH, SEQ, D), 1.0),
        "in_ln_g": 1.0 + nrm(ks[1], (D,), 0.01),
        "in_ln_b": nrm(ks[2], (D,), 0.01),
        "w_in": nrm(ks[3], (L, D, IN_WIDTH), D ** -0.5),
        "b_in": nrm(ks[4], (L, IN_WIDTH), 0.01),
        "pool_w": nrm(ks[5], (L, POOL_GROUPS, POOL_GROUP_DIM, POOL_GROUP_DIM), POOL_GROUP_DIM ** -0.5),
        "pool_scale": 1.0 + nrm(ks[6], (L, POOL_WIDTH), 0.1),
        "sgu_ln_g": 1.0 + nrm(ks[7], (L, SGU_WIDTH), 0.01),
        "sgu_ln_b": nrm(ks[8], (L, SGU_WIDTH), 0.01),
        "sgu_w": nrm(ks[9], (L, SGU_HEADS, SGU_CHUNK, SGU_CHUNK), 0.5 * SGU_CHUNK ** -0.5),
        "sgu_b": 1.0 + nrm(ks[10], (L, SGU_HEADS, SGU_CHUNK), 0.1),
        "p_a": nrm(ks[11], (L, POOL_WIDTH, D), BETA * POOL_WIDTH ** -0.5),
        "p_b": nrm(ks[12], (L, SGU_WIDTH, D), BETA * SGU_WIDTH ** -0.5),
        "w_out": nrm(ks[13], (L, D, D), BETA * D ** -0.5),
        "ln1_g": 1.0 + nrm(ks[14], (L, D), 0.01),
        "ln1_b": nrm(ks[15], (L, D), 0.01),
        "w_router": nrm(ks[16], (L, D, N_EXPERTS), D ** -0.5),
        "w_gate": nrm(ks[17], (L, N_EXPERTS, D, D_EXPERT), D ** -0.5),
        "w_up": nrm(ks[18], (L, N_EXPERTS, D, D_EXPERT), D ** -0.5),
        "w_down": nrm(ks[19], (L, N_EXPERTS, D_EXPERT, D), BETA * D_EXPERT ** -0.5),
        "ln2_g": 1.0 + nrm(ks[20], (L, D), 0.01),
        "ln2_b": nrm(ks[21], (L, D), 0.01),
    }


def reference(x, in_ln_g, in_ln_b, w_in, b_in, pool_w, pool_scale, sgu_ln_g, sgu_ln_b, sgu_w, sgu_b,
              p_a, p_b, w_out, ln1_g, ln1_b, w_router, w_gate, w_up, w_down, ln2_g, ln2_b):
    h = layer_norm(x, in_ln_g, in_ln_b)
    for l in range(DEPTH):
        mix = hybrid_mixer(h, w_in[l], b_in[l], pool_w[l], pool_scale[l], sgu_ln_g[l], sgu_ln_b[l],
                           sgu_w[l], sgu_b[l], p_a[l], p_b[l], w_out[l])
        h = layer_norm(ALPHA * h + mix, ln1_g[l], ln1_b[l])
        ffn = expert_choice_ffn(h, w_router[l], w_gate[l], w_up[l], w_down[l])
        h = layer_norm(ALPHA * h + ffn, ln2_g[l], ln2_b[l])
    return h
```

```python
import functools

import jax
import jax.numpy as jnp
from jax import lax
from jax.experimental import pallas as pl
from jax.experimental.pallas import tpu as pltpu

F32 = jnp.float32
BF16 = jnp.bfloat16

POOL_WINDOWS = (2, 4, 8, 16)
POOL_HALO = 8
SGU_CHUNK = 128
SGU_HEADS = 8
CAPACITY_FACTOR = 2
LN_EPS = 1e-5
LANES = 128
SUBLANES = 8
VMEM_LIMIT = 56 * 1024 * 1024

MIX_TM = 512
MOE_FC = 512
LN_TM = 512


def _ln(x, g, b):
    mu = jnp.mean(x, axis=-1, keepdims=True)
    xc = x - mu
    var = jnp.mean(xc * xc, axis=-1, keepdims=True)
    return xc * lax.rsqrt(var + LN_EPS) * g + b


def _dot(a, b):
    return jnp.dot(a, b, preferred_element_type=F32)


def _mixer_kernel(apply_in_ln, seq, alpha,
                  x_ref, xp_ref, xn_ref, ing_ref, inb_ref, win_ref, bin_ref, poolw_ref, pscale_ref,
                  sg_ref, sb_ref, sw_ref, sbias_ref, pa_ref, pb_ref, wout_ref, l1g_ref, l1b_ref, wrt_ref,
                  h_ref, logit_ref,
                  proj_ref, aext_ref, ya_ref, yb_ref):
    tm, d = x_ref.shape
    pw = ya_ref.shape[1]
    gd = pw // len(POOL_WINDOWS)
    sw = yb_ref.shape[1]
    hd = sw // SGU_HEADS
    o_u, o_v, o_ga, o_gb = pw, pw + sw, pw + 2 * sw, pw + 2 * sw + d

    i = pl.program_id(0)
    tiles_per_seq = seq // tm
    pos = i % tiles_per_seq

    x = x_ref[...]
    xp = xp_ref[...]
    xn = xn_ref[...]
    if apply_in_ln:
        x = _ln(x, ing_ref[...], inb_ref[...])
        xp = _ln(xp, ing_ref[...], inb_ref[...])
        xn = _ln(xn, ing_ref[...], inb_ref[...])
    xb = x.astype(BF16)

    proj_ref[...] = _dot(xb, win_ref[...]) + bin_ref[...]
    ap = _dot(xp.astype(BF16), win_ref[:, 0:pw]) + bin_ref[:, 0:pw]
    an = _dot(xn.astype(BF16), win_ref[:, 0:pw]) + bin_ref[:, 0:pw]
    ap = jnp.where(pos == 0, 0.0, ap)
    an = jnp.where(pos == tiles_per_seq - 1, 0.0, an)
    aext_ref[0:POOL_HALO, :] = ap
    aext_ref[POOL_HALO:POOL_HALO + tm, :] = proj_ref[:, 0:pw]
    aext_ref[POOL_HALO + tm:2 * POOL_HALO + tm, :] = an

    for c in range(tm // SGU_CHUNK):
        r0 = c * SGU_CHUNK
        rows = pl.ds(r0, SGU_CHUNK)
        s = pos * tm + r0 + lax.broadcasted_iota(jnp.int32, (SGU_CHUNK, 1), 0)
        for g, w in enumerate(POOL_WINDOWS):
            cols = slice(g * gd, (g + 1) * gd)
            acc = aext_ref[pl.ds(POOL_HALO + r0 - w // 2, SGU_CHUNK), cols]
            for o in range(-w // 2 + 1, w // 2):
                acc = acc + aext_ref[pl.ds(POOL_HALO + r0 + o, SGU_CHUNK), cols]
            cnt = (jnp.minimum(s + w // 2, seq) - jnp.maximum(s - w // 2, 0)).astype(F32)
            pooled = acc / cnt - proj_ref[rows, cols]
            ya = _dot(pooled.astype(BF16), poolw_ref[g]) * pscale_ref[:, cols]
            ya_ref[rows, cols] = ya.astype(BF16)
        gu = jax.nn.gelu(proj_ref[rows, o_u:o_v])
        gv = jax.nn.gelu(proj_ref[rows, o_v:o_ga])
        vb = _ln(gv, sg_ref[...], sb_ref[...]).astype(BF16)
        for hh in range(SGU_HEADS):
            hc = slice(hh * hd, (hh + 1) * hd)
            mixed = _dot(sw_ref[hh], vb[:, hc]) + sbias_ref[:, hc]
            yb_ref[rows, hc] = (gu[:, hc] * mixed).astype(BF16)

    ta = _dot(ya_ref[...], pa_ref[...])
    tb = _dot(yb_ref[...], pb_ref[...])
    merged = (jax.nn.sigmoid(proj_ref[:, o_ga:o_gb]) * ta
              + jax.nn.sigmoid(proj_ref[:, o_gb:o_gb + d]) * tb)
    mix = _dot(merged.astype(BF16), wout_ref[...])
    h1 = _ln(alpha * x + mix, l1g_ref[...], l1b_ref[...])
    h_ref[...] = h1
    logit_ref[...] = lax.dot_general(wrt_ref[...], h1.astype(BF16), (((1,), (1,)), ((), ())),
                                     preferred_element_type=F32)


def _mixer(x2d, seq, alpha, apply_in_ln, in_g, in_b, w_in, b_in, pool_w, pool_scale, sgu_g, sgu_b, sgu_w,
           sgu_bias, p_a, p_b, w_out, ln1_g, ln1_b, w_rt):
    t, d = x2d.shape
    tm = MIX_TM
    n_e = w_rt.shape[0]
    pw = p_a.shape[0]
    sw = p_b.shape[0]
    hb = tm // POOL_HALO
    n_hb = t // POOL_HALO

    def full(a):
        nd = a.ndim
        return pl.BlockSpec(a.shape, lambda i, _n=nd: (0,) * _n)

    weights = (in_g, in_b, w_in, b_in, pool_w, pool_scale, sgu_g, sgu_b, sgu_w, sgu_bias, p_a, p_b, w_out,
               ln1_g, ln1_b, w_rt)
    return pl.pallas_call(
        functools.partial(_mixer_kernel, apply_in_ln, seq, alpha),
        grid=(t // tm,),
        in_specs=[
            pl.BlockSpec((tm, d), lambda i: (i, 0)),
            pl.BlockSpec((POOL_HALO, d), lambda i: (jnp.maximum(i * hb - 1, 0), 0)),
            pl.BlockSpec((POOL_HALO, d), lambda i: (jnp.minimum((i + 1) * hb, n_hb - 1), 0)),
        ] + [full(a) for a in weights],
        out_specs=[
            pl.BlockSpec((tm, d), lambda i: (i, 0)),
            pl.BlockSpec((n_e, tm), lambda i: (0, i)),
        ],
        out_shape=[
            jax.ShapeDtypeStruct((t, d), F32),
            jax.ShapeDtypeStruct((n_e, t), F32),
        ],
        scratch_shapes=[
            pltpu.VMEM((tm, w_in.shape[1]), F32),
            pltpu.VMEM((tm + 2 * POOL_HALO, pw), F32),
            pltpu.VMEM((tm, pw), BF16),
            pltpu.VMEM((tm, sw), BF16),
        ],
        compiler_params=pltpu.CompilerParams(
            dimension_semantics=("arbitrary",), vmem_limit_bytes=VMEM_LIMIT),
        name="mixer",
    )(x2d, x2d, x2d, *weights)


def _route_kernel(cap, logit_ref, idx_ref, gate_ref, csel_ref, aff_ref):
    n_e, seq = logit_ref.shape
    n_tiles = seq // LANES

    lg = logit_ref[...]
    ex = jnp.exp(lg - jnp.max(lg, axis=0, keepdims=True))
    aff = ex / jnp.sum(ex, axis=0, keepdims=True)

    def bit_step(k, thr):
        cand = thr | jnp.left_shift(jnp.int32(1), 30 - k)
        cand_f = pltpu.bitcast(cand, F32)
        cnt = jnp.sum(jnp.where(aff >= cand_f, 1.0, 0.0), axis=1, keepdims=True)
        return jnp.where(cnt >= cap, cand, thr)

    thr = lax.fori_loop(0, 31, bit_step, jnp.zeros((n_e, 1), jnp.int32))
    thr_f = pltpu.bitcast(thr, F32)
    gt = aff > thr_f
    eq = aff == thr_f
    need = cap - jnp.sum(jnp.where(gt, 1.0, 0.0), axis=1, keepdims=True)

    tri = (lax.broadcasted_iota(jnp.int32, (LANES, LANES), 0)
           <= lax.broadcasted_iota(jnp.int32, (LANES, LANES), 1)).astype(BF16)
    carry_gt = jnp.zeros((n_e, 1), F32)
    carry_eq = jnp.zeros((n_e, 1), F32)
    for k in range(n_tiles):
        cols = slice(k * LANES, (k + 1) * LANES)
        gt_k = gt[:, cols]
        eq_k = eq[:, cols]
        c_gt = _dot(jnp.where(gt_k, 1.0, 0.0).astype(BF16), tri) + carry_gt
        c_eq = _dot(jnp.where(eq_k, 1.0, 0.0).astype(BF16), tri) + carry_eq
        carry_gt = c_gt[:, LANES - 1:LANES]
        carry_eq = c_eq[:, LANES - 1:LANES]
        sel_k = gt_k | (eq_k & (c_eq <= need))
        rank = c_gt + jnp.minimum(c_eq, need)
        csel_ref[k] = jnp.where(sel_k, rank, 0.0)
        aff_ref[k] = aff[:, cols]

    slot = (lax.broadcasted_iota(jnp.int32, (cap, LANES), 0) + 1).astype(F32)
    lane = lax.broadcasted_iota(jnp.int32, (1, LANES), 1).astype(F32)
    ones = jnp.ones((SUBLANES, LANES), F32)

    def per_expert(e, _):
        def per_tile(k, accs):
            acc_t, acc_g = accs
            hit = csel_ref[k, pl.ds(e, 1), :] == slot
            tok = lane + (k * LANES).astype(F32)
            acc_t = acc_t + jnp.where(hit, tok, 0.0)
            acc_g = acc_g + jnp.where(hit, aff_ref[k, pl.ds(e, 1), :], 0.0)
            return acc_t, acc_g

        zero = jnp.zeros((cap, LANES), F32)
        acc_t, acc_g = lax.fori_loop(0, n_tiles, per_tile, (zero, zero))
        dn = (((1,), (1,)), ((), ()))
        tok_row = lax.dot_general(ones, acc_t, dn, precision=lax.Precision.HIGHEST,
                                  preferred_element_type=F32)
        gate_row = lax.dot_general(ones, acc_g, dn, precision=lax.Precision.HIGHEST,
                                   preferred_element_type=F32)
        idx_ref[0, pl.ds(e, 1), :] = tok_row[0:1, :].astype(jnp.int32)
        gate_ref[0, pl.ds(e, 1), :] = gate_row[0:1, :]
        return 0

    lax.fori_loop(0, n_e, per_expert, 0)


def _route(logits_t, n_batch, seq, cap):
    n_e = logits_t.shape[0]
    return pl.pallas_call(
        functools.partial(_route_kernel, cap),
        grid=(n_batch,),
        in_specs=[pl.BlockSpec((n_e, seq), lambda b: (0, b))],
        out_specs=[
            pl.BlockSpec((1, n_e, cap), lambda b: (b, 0, 0)),
            pl.BlockSpec((1, n_e, cap), lambda b: (b, 0, 0)),
        ],
        out_shape=[
            jax.ShapeDtypeStruct((n_batch, n_e, cap), jnp.int32),
            jax.ShapeDtypeStruct((n_batch, n_e, cap), F32),
        ],
        scratch_shapes=[pltpu.VMEM((seq // LANES, n_e, LANES), F32),
                        pltpu.VMEM((seq // LANES, n_e, LANES), F32)],
        compiler_params=pltpu.CompilerParams(
            dimension_semantics=("arbitrary",), vmem_limit_bytes=VMEM_LIMIT),
        name="route",
    )(logits_t)


def _moe_kernel(row_stride, idx_ref, gate_ref, h_ref, wg_ref, wu_ref, wd_ref, out_ref, xe_ref, xt_ref, yt_ref):
    cap, d = xe_ref.shape
    rpt = d // LANES
    e = pl.program_id(1)
    f = pl.program_id(2)
    n_f = pl.num_programs(2)
    unroll = 8

    @pl.when((e == 0) & (f == 0))
    def _():
        out_ref[...] = jnp.zeros_like(out_ref)

    @pl.when(f == 0)
    def _():
        def gather(jj, _):
            for u in range(unroll):
                j = jj * unroll + u
                r = pl.multiple_of(idx_ref[0, 0, j] * rpt, rpt)
                xt_ref[pl.ds(j, rpt, stride=row_stride), :] = h_ref[0, pl.ds(r, rpt), :]
            return 0

        lax.fori_loop(0, cap // unroll, gather, 0)
        for q in range(rpt):
            xe_ref[:, q * LANES:(q + 1) * LANES] = xt_ref[pl.ds(q * row_stride, cap), :].astype(BF16)

    x = xe_ref[...]
    g = _dot(x, wg_ref[0])
    u = _dot(x, wu_ref[0])
    hid = (jax.nn.silu(g) * u).astype(BF16)
    y = _dot(hid, wd_ref[0])

    @pl.when(f == 0)
    def _():
        for q in range(rpt):
            yt_ref[pl.ds(q * row_stride, cap), :] = y[:, q * LANES:(q + 1) * LANES]

    @pl.when(f > 0)
    def _():
        for q in range(rpt):
            yt_ref[pl.ds(q * row_stride, cap), :] += y[:, q * LANES:(q + 1) * LANES]

    @pl.when(f == n_f - 1)
    def _():
        def scatter(jj, _):
            rows = []
            vals = []
            for u in range(unroll):
                j = jj * unroll + u
                r = pl.multiple_of(idx_ref[0, 0, j] * rpt, rpt)
                slab = yt_ref[pl.ds(j, rpt, stride=row_stride), :] * gate_ref[0, 0, j]
                rows.append(r)
                vals.append(out_ref[0, pl.ds(r, rpt), :] + slab)
            for r, v in zip(rows, vals):
                out_ref[0, pl.ds(r, rpt), :] = v
            return 0

        lax.fori_loop(0, cap // unroll, scatter, 0)


def _moe(h3, idx, gates, w_gate, w_up, w_down):
    n_batch, rows, _ = h3.shape
    n_e, d, d_exp = w_gate.shape
    cap = idx.shape[-1]
    fc = MOE_FC
    rpt = d // LANES
    row_stride = cap + SUBLANES
    idx3 = idx.reshape(n_batch * n_e, 1, cap)
    gates3 = gates.reshape(n_batch * n_e, 1, cap)
    return pl.pallas_call(
        functools.partial(_moe_kernel, row_stride),
        grid=(n_batch, n_e, d_exp // fc),
        in_specs=[
            pl.BlockSpec((1, 1, cap), lambda b, e, f: (b * n_e + e, 0, 0), memory_space=pltpu.SMEM),
            pl.BlockSpec((1, 1, cap), lambda b, e, f: (b * n_e + e, 0, 0), memory_space=pltpu.SMEM),
            pl.BlockSpec((1, rows, LANES), lambda b, e, f: (b, 0, 0), pipeline_mode=pl.Buffered(1)),
            pl.BlockSpec((1, d, fc), lambda b, e, f: (e, 0, f)),
            pl.BlockSpec((1, d, fc), lambda b, e, f: (e, 0, f)),
            pl.BlockSpec((1, fc, d), lambda b, e, f: (e, f, 0)),
        ],
        out_specs=pl.BlockSpec((1, rows, LANES), lambda b, e, f: (b, 0, 0), pipeline_mode=pl.Buffered(1)),
        out_shape=jax.ShapeDtypeStruct((n_batch, rows, LANES), F32),
        scratch_shapes=[
            pltpu.VMEM((cap, d), BF16),
            pltpu.VMEM((rpt * row_stride, LANES), F32),
            pltpu.VMEM((rpt * row_stride, LANES), F32),
        ],
        compiler_params=pltpu.CompilerParams(
            dimension_semantics=("arbitrary", "arbitrary", "arbitrary"), vmem_limit_bytes=VMEM_LIMIT),
        name="moe",
    )(idx3, gates3, h3, w_gate, w_up, w_down)


def _ln2_kernel(alpha, h_ref, f_ref, g_ref, b_ref, o_ref):
    o_ref[...] = _ln(alpha * h_ref[...] + f_ref[...], g_ref[...], b_ref[...])


def _ln2(h2d, ffn2d, alpha, g, b):
    t, d = h2d.shape
    tm = LN_TM
    return pl.pallas_call(
        functools.partial(_ln2_kernel, alpha),
        grid=(t // tm,),
        in_specs=[
            pl.BlockSpec((tm, d), lambda i: (i, 0)),
            pl.BlockSpec((tm, d), lambda i: (i, 0)),
            pl.BlockSpec((1, d), lambda i: (0, 0)),
            pl.BlockSpec((1, d), lambda i: (0, 0)),
        ],
        out_specs=pl.BlockSpec((tm, d), lambda i: (i, 0)),
        out_shape=jax.ShapeDtypeStruct((t, d), F32),
        compiler_params=pltpu.CompilerParams(dimension_semantics=("arbitrary",)),
        name="ln2",
    )(h2d, ffn2d, g, b)


def kernel(x, in_ln_g, in_ln_b, w_in, b_in, pool_w, pool_scale, sgu_ln_g, sgu_ln_b, sgu_w, sgu_b, p_a, p_b,
           w_out, ln1_g, ln1_b, w_router, w_gate, w_up, w_down, ln2_g, ln2_b):
    n_batch, seq, d = x.shape
    depth = w_in.shape[0]
    n_e = w_router.shape[-1]
    cap = CAPACITY_FACTOR * seq // n_e
    alpha = (2 * depth) ** 0.25
    t = n_batch * seq
    hd = sgu_ln_g.shape[-1] // SGU_HEADS

    def row(a):
        return a.reshape(1, -1)

    h = x.reshape(t, d)
    for l in range(depth):
        sgu_bias = jnp.repeat(sgu_b[l].T, hd, axis=1)
        h1, logits_t = _mixer(
            h, seq, alpha, l == 0, row(in_ln_g), row(in_ln_b),
            w_in[l].astype(BF16), row(b_in[l]), pool_w[l].astype(BF16), row(pool_scale[l]),
            row(sgu_ln_g[l]), row(sgu_ln_b[l]), sgu_w[l].astype(BF16), sgu_bias,
            p_a[l].astype(BF16), p_b[l].astype(BF16), w_out[l].astype(BF16),
            row(ln1_g[l]), row(ln1_b[l]), w_router[l].T.astype(BF16))
        idx, gates = _route(logits_t, n_batch, seq, cap)
        ffn = _moe(h1.reshape(n_batch, seq * d // LANES, LANES), idx, gates,
                   w_gate[l].astype(BF16), w_up[l].astype(BF16), w_down[l].astype(BF16))
        h = _ln2(h1, ffn.reshape(t, d), alpha, row(ln2_g[l]), row(ln2_b[l]))
    return h.reshape(n_batch, seq, d)
```

```python
import functools

import jax
import jax.numpy as jnp
from jax import lax
from jax.experimental import pallas as pl
from jax.experimental.pallas import tpu as pltpu

F32 = jnp.float32
BF16 = jnp.bfloat16

POOL_WINDOWS = (2, 4, 8, 16)
POOL_HALO = 8
SGU_CHUNK = 128
SGU_HEADS = 8
CAPACITY_FACTOR = 2
LN_EPS = 1e-5
LANES = 128
SUBLANES = 8
BF16_ROWS = 16
VMEM_LIMIT = 56 * 1024 * 1024

MIX_TM = 512
MOE_FC = 512
LN_TM = 512


def _ln(x, g, b):
    mu = jnp.mean(x, axis=-1, keepdims=True)
    xc = x - mu
    var = jnp.mean(xc * xc, axis=-1, keepdims=True)
    return xc * lax.rsqrt(var + LN_EPS) * g + b


def _dot(a, b):
    return jnp.dot(a, b, preferred_element_type=F32)


def _mixer_kernel(apply_in_ln, seq, alpha,
                  x_ref, xp_ref, xn_ref, ing_ref, inb_ref, win_ref, bin_ref, poolw_ref, pscale_ref,
                  sg_ref, sb_ref, sw_ref, sbias_ref, pa_ref, pb_ref, wout_ref, l1g_ref, l1b_ref, wrt_ref,
                  h_ref, logit_ref,
                  proj_ref, aext_ref, ya_ref, yb_ref):
    tm, d = x_ref.shape
    pw = ya_ref.shape[1]
    gd = pw // len(POOL_WINDOWS)
    sw = yb_ref.shape[1]
    hd = sw // SGU_HEADS
    o_u, o_v, o_ga, o_gb = pw, pw + sw, pw + 2 * sw, pw + 2 * sw + d

    i = pl.program_id(0)
    tiles_per_seq = seq // tm
    pos = i % tiles_per_seq

    x = x_ref[...]
    xp = xp_ref[...]
    xn = xn_ref[...]
    if apply_in_ln:
        x = _ln(x, ing_ref[...], inb_ref[...])
        xp = _ln(xp, ing_ref[...], inb_ref[...])
        xn = _ln(xn, ing_ref[...], inb_ref[...])
    xb = x.astype(BF16)

    proj_ref[...] = _dot(xb, win_ref[...]) + bin_ref[...]
    ap = _dot(xp.astype(BF16), win_ref[:, 0:pw]) + bin_ref[:, 0:pw]
    an = _dot(xn.astype(BF16), win_ref[:, 0:pw]) + bin_ref[:, 0:pw]
    ap = jnp.where(pos == 0, 0.0, ap)
    an = jnp.where(pos == tiles_per_seq - 1, 0.0, an)
    aext_ref[0:POOL_HALO, :] = ap
    aext_ref[POOL_HALO:POOL_HALO + tm, :] = proj_ref[:, 0:pw]
    aext_ref[POOL_HALO + tm:2 * POOL_HALO + tm, :] = an

    for c in range(tm // SGU_CHUNK):
        r0 = c * SGU_CHUNK
        rows = pl.ds(r0, SGU_CHUNK)
        s = pos * tm + r0 + lax.broadcasted_iota(jnp.int32, (SGU_CHUNK, 1), 0)
        for g, w in enumerate(POOL_WINDOWS):
            cols = slice(g * gd, (g + 1) * gd)
            acc = aext_ref[pl.ds(POOL_HALO + r0 - w // 2, SGU_CHUNK), cols]
            for o in range(-w // 2 + 1, w // 2):
                acc = acc + aext_ref[pl.ds(POOL_HALO + r0 + o, SGU_CHUNK), cols]
            cnt = (jnp.minimum(s + w // 2, seq) - jnp.maximum(s - w // 2, 0)).astype(F32)
            pooled = acc / cnt - proj_ref[rows, cols]
            ya = _dot(pooled.astype(BF16), poolw_ref[g]) * pscale_ref[:, cols]
            ya_ref[rows, cols] = ya.astype(BF16)
        gu = jax.nn.gelu(proj_ref[rows, o_u:o_v])
        gv = jax.nn.gelu(proj_ref[rows, o_v:o_ga])
        vb = _ln(gv, sg_ref[...], sb_ref[...]).astype(BF16)
        for hh in range(SGU_HEADS):
            hc = slice(hh * hd, (hh + 1) * hd)
            mixed = _dot(sw_ref[hh], vb[:, hc]) + sbias_ref[:, hc]
            yb_ref[rows, hc] = (gu[:, hc] * mixed).astype(BF16)

    ta = _dot(ya_ref[...], pa_ref[...])
    tb = _dot(yb_ref[...], pb_ref[...])
    merged = (jax.nn.sigmoid(proj_ref[:, o_ga:o_gb]) * ta
              + jax.nn.sigmoid(proj_ref[:, o_gb:o_gb + d]) * tb)
    mix = _dot(merged.astype(BF16), wout_ref[...])
    h1 = _ln(alpha * x + mix, l1g_ref[...], l1b_ref[...])
    h_ref[...] = h1
    logit_ref[...] = lax.dot_general(wrt_ref[...], h1.astype(BF16), (((1,), (1,)), ((), ())),
                                     preferred_element_type=F32)


def _mixer(x2d, seq, alpha, apply_in_ln, in_g, in_b, w_in, b_in, pool_w, pool_scale, sgu_g, sgu_b, sgu_w,
           sgu_bias, p_a, p_b, w_out, ln1_g, ln1_b, w_rt):
    t, d = x2d.shape
    tm = MIX_TM
    n_e = w_rt.shape[0]
    pw = p_a.shape[0]
    sw = p_b.shape[0]
    hb = tm // POOL_HALO
    n_hb = t // POOL_HALO

    def full(a):
        nd = a.ndim
        return pl.BlockSpec(a.shape, lambda i, _n=nd: (0,) * _n)

    weights = (in_g, in_b, w_in, b_in, pool_w, pool_scale, sgu_g, sgu_b, sgu_w, sgu_bias, p_a, p_b, w_out,
               ln1_g, ln1_b, w_rt)
    return pl.pallas_call(
        functools.partial(_mixer_kernel, apply_in_ln, seq, alpha),
        grid=(t // tm,),
        in_specs=[
            pl.BlockSpec((tm, d), lambda i: (i, 0)),
            pl.BlockSpec((POOL_HALO, d), lambda i: (jnp.maximum(i * hb - 1, 0), 0)),
            pl.BlockSpec((POOL_HALO, d), lambda i: (jnp.minimum((i + 1) * hb, n_hb - 1), 0)),
        ] + [full(a) for a in weights],
        out_specs=[
            pl.BlockSpec((tm, d), lambda i: (i, 0)),
            pl.BlockSpec((n_e, tm), lambda i: (0, i)),
        ],
        out_shape=[
            jax.ShapeDtypeStruct((t, d), F32),
            jax.ShapeDtypeStruct((n_e, t), F32),
        ],
        scratch_shapes=[
            pltpu.VMEM((tm, w_in.shape[1]), F32),
            pltpu.VMEM((tm + 2 * POOL_HALO, pw), F32),
            pltpu.VMEM((tm, pw), BF16),
            pltpu.VMEM((tm, sw), BF16),
        ],
        compiler_params=pltpu.CompilerParams(
            dimension_semantics=("arbitrary",), vmem_limit_bytes=VMEM_LIMIT),
        name="mixer",
    )(x2d, x2d, x2d, *weights)


def _route_kernel(cap, logit_ref, idx_ref, gate_ref, rank_ref, val_ref, hot_ref, res_ref):
    n_e, seq = logit_ref.shape
    n_tiles = seq // LANES
    tok_shift = 6
    chunk_tiles = 4

    lg = logit_ref[...]
    ex = jnp.exp(lg - jnp.max(lg, axis=0, keepdims=True))
    aff = ex / jnp.sum(ex, axis=0, keepdims=True)

    def bit_step(k, thr):
        cand = thr | jnp.left_shift(jnp.int32(1), 30 - k)
        cand_f = pltpu.bitcast(cand, F32)
        cnt = jnp.sum(jnp.where(aff >= cand_f, 1.0, 0.0), axis=1, keepdims=True)
        return jnp.where(cnt >= cap, cand, thr)

    thr = lax.fori_loop(0, 31, bit_step, jnp.zeros((n_e, 1), jnp.int32))
    thr_f = pltpu.bitcast(thr, F32)
    gt = aff > thr_f
    eq = aff == thr_f
    need = cap - jnp.sum(jnp.where(gt, 1.0, 0.0), axis=1, keepdims=True)

    tri = (lax.broadcasted_iota(jnp.int32, (LANES, LANES), 0)
           <= lax.broadcasted_iota(jnp.int32, (LANES, LANES), 1)).astype(BF16)
    carry_gt = jnp.zeros((n_e, 1), F32)
    carry_eq = jnp.zeros((n_e, 1), F32)
    for k in range(n_tiles):
        cols = slice(k * LANES, (k + 1) * LANES)
        gt_k = gt[:, cols]
        eq_k = eq[:, cols]
        c_gt = _dot(jnp.where(gt_k, 1.0, 0.0).astype(BF16), tri) + carry_gt
        c_eq = _dot(jnp.where(eq_k, 1.0, 0.0).astype(BF16), tri) + carry_eq
        carry_gt = c_gt[:, LANES - 1:LANES]
        carry_eq = c_eq[:, LANES - 1:LANES]
        sel_k = gt_k | (eq_k & (c_eq <= need))
        rank = c_gt + jnp.minimum(c_eq, need)
        rank_ref[k] = jnp.where(sel_k, rank, 0.0)

    hi = aff.astype(BF16)
    rest = aff - hi.astype(F32)
    mid = rest.astype(BF16)
    lo = (rest - mid.astype(F32)).astype(BF16)
    tok = lax.broadcasted_iota(jnp.int32, (BF16_ROWS, seq), 1)
    r_hi, r_lo = 3 * n_e, 3 * n_e + BF16_ROWS
    val_ref[0 * n_e:1 * n_e, :] = hi
    val_ref[1 * n_e:2 * n_e, :] = mid
    val_ref[2 * n_e:3 * n_e, :] = lo
    val_ref[r_hi:r_hi + BF16_ROWS, :] = lax.shift_right_logical(tok, tok_shift).astype(F32).astype(BF16)
    val_ref[r_lo:r_lo + BF16_ROWS, :] = (tok & ((1 << tok_shift) - 1)).astype(F32).astype(BF16)

    slot = (lax.broadcasted_iota(jnp.int32, (cap, LANES), 0) + 1).astype(F32)

    def per_expert(e, _):
        acc = jnp.zeros(res_ref.shape, F32)
        for c in range(n_tiles // chunk_tiles):
            for k in range(c * chunk_tiles, (c + 1) * chunk_tiles):
                hit = rank_ref[k, pl.ds(e, 1), :] == slot
                hot_ref[:, k * LANES:(k + 1) * LANES] = jnp.where(hit, 1.0, 0.0).astype(BF16)
            cols = slice(c * chunk_tiles * LANES, (c + 1) * chunk_tiles * LANES)
            acc = acc + lax.dot_general(val_ref[:, cols], hot_ref[:, cols], (((1,), (1,)), ((), ())),
                                        preferred_element_type=F32)
        res_ref[...] = acc
        tok_row = res_ref[r_hi:r_hi + 1, :] * float(1 << tok_shift) + res_ref[r_lo:r_lo + 1, :]
        gate_row = (res_ref[pl.ds(e, 1), :] + res_ref[pl.ds(n_e + e, 1), :]) + res_ref[pl.ds(2 * n_e + e, 1), :]
        idx_ref[0, pl.ds(e, 1), :] = tok_row.astype(jnp.int32)
        gate_ref[0, pl.ds(e, 1), :] = gate_row
        return 0

    lax.fori_loop(0, n_e, per_expert, 0)


def _route(logits_t, n_batch, seq, cap):
    n_e = logits_t.shape[0]
    return pl.pallas_call(
        functools.partial(_route_kernel, cap),
        grid=(n_batch,),
        in_specs=[pl.BlockSpec((n_e, seq), lambda b: (0, b))],
        out_specs=[
            pl.BlockSpec((1, n_e, cap), lambda b: (b, 0, 0)),
            pl.BlockSpec((1, n_e, cap), lambda b: (b, 0, 0)),
        ],
        out_shape=[
            jax.ShapeDtypeStruct((n_batch, n_e, cap), jnp.int32),
            jax.ShapeDtypeStruct((n_batch, n_e, cap), F32),
        ],
        scratch_shapes=[pltpu.VMEM((seq // LANES, n_e, LANES), F32),
                        pltpu.VMEM((3 * n_e + 2 * BF16_ROWS, seq), BF16),
                        pltpu.VMEM((cap, seq), BF16),
                        pltpu.VMEM((3 * n_e + 2 * BF16_ROWS, cap), F32)],
        compiler_params=pltpu.CompilerParams(
            dimension_semantics=("arbitrary",), vmem_limit_bytes=VMEM_LIMIT),
        name="route",
    )(logits_t)


def _moe_kernel(row_stride, idx_ref, gate_ref, h_ref, wg_ref, wu_ref, wd_ref, out_ref, xe_ref, xt_ref, yt_ref):
    cap, d = xe_ref.shape
    rpt = d // LANES
    e = pl.program_id(1)
    f = pl.program_id(2)
    n_f = pl.num_programs(2)
    unroll = 8

    @pl.when((e == 0) & (f == 0))
    def _():
        out_ref[...] = jnp.zeros_like(out_ref)

    @pl.when(f == 0)
    def _():
        def gather(jj, _):
            for u in range(unroll):
                j = jj * unroll + u
                r = pl.multiple_of(idx_ref[0, 0, j] * rpt, rpt)
                xt_ref[pl.ds(j, rpt, stride=row_stride), :] = h_ref[0, pl.ds(r, rpt), :]
            return 0

        lax.fori_loop(0, cap // unroll, gather, 0)
        for q in range(rpt):
            xe_ref[:, q * LANES:(q + 1) * LANES] = xt_ref[pl.ds(q * row_stride, cap), :].astype(BF16)

    x = xe_ref[...]
    g = _dot(x, wg_ref[0, 0])
    u = _dot(x, wu_ref[0, 0])
    hid = (jax.nn.silu(g) * u).astype(BF16)
    y = _dot(hid, wd_ref[0, 0])

    @pl.when(f == 0)
    def _():
        for q in range(rpt):
            yt_ref[pl.ds(q * row_stride, cap), :] = y[:, q * LANES:(q + 1) * LANES]

    @pl.when(f > 0)
    def _():
        for q in range(rpt):
            yt_ref[pl.ds(q * row_stride, cap), :] += y[:, q * LANES:(q + 1) * LANES]

    @pl.when(f == n_f - 1)
    def _():
        def scatter(jj, _):
            rows = []
            vals = []
            for u in range(unroll):
                j = jj * unroll + u
                r = pl.multiple_of(idx_ref[0, 0, j] * rpt, rpt)
                slab = yt_ref[pl.ds(j, rpt, stride=row_stride), :] * gate_ref[0, 0, j]
                rows.append(r)
                vals.append(out_ref[0, pl.ds(r, rpt), :] + slab)
            for r, v in zip(rows, vals):
                out_ref[0, pl.ds(r, rpt), :] = v
            return 0

        lax.fori_loop(0, cap // unroll, scatter, 0)


def _moe(h3, idx, gates, layer, w_gate, w_up, w_down):
    n_batch, rows, _ = h3.shape
    _, n_e, d, d_exp = w_gate.shape
    cap = idx.shape[-1]
    fc = MOE_FC
    rpt = d // LANES
    row_stride = cap + SUBLANES
    idx3 = idx.reshape(n_batch * n_e, 1, cap)
    gates3 = gates.reshape(n_batch * n_e, 1, cap)
    return pl.pallas_call(
        functools.partial(_moe_kernel, row_stride),
        grid=(n_batch, n_e, d_exp // fc),
        in_specs=[
            pl.BlockSpec((1, 1, cap), lambda b, e, f: (b * n_e + e, 0, 0), memory_space=pltpu.SMEM),
            pl.BlockSpec((1, 1, cap), lambda b, e, f: (b * n_e + e, 0, 0), memory_space=pltpu.SMEM),
            pl.BlockSpec((1, rows, LANES), lambda b, e, f: (b, 0, 0), pipeline_mode=pl.Buffered(1)),
            pl.BlockSpec((1, 1, d, fc), lambda b, e, f: (layer, e, 0, f)),
            pl.BlockSpec((1, 1, d, fc), lambda b, e, f: (layer, e, 0, f)),
            pl.BlockSpec((1, 1, fc, d), lambda b, e, f: (layer, e, f, 0)),
        ],
        out_specs=pl.BlockSpec((1, rows, LANES), lambda b, e, f: (b, 0, 0), pipeline_mode=pl.Buffered(1)),
        out_shape=jax.ShapeDtypeStruct((n_batch, rows, LANES), F32),
        scratch_shapes=[
            pltpu.VMEM((cap, d), BF16),
            pltpu.VMEM((rpt * row_stride, LANES), F32),
            pltpu.VMEM((rpt * row_stride, LANES), F32),
        ],
        compiler_params=pltpu.CompilerParams(
            dimension_semantics=("arbitrary", "arbitrary", "arbitrary"), vmem_limit_bytes=VMEM_LIMIT),
        name="moe",
    )(idx3, gates3, h3, w_gate, w_up, w_down)


def _ln2_kernel(alpha, h_ref, f_ref, g_ref, b_ref, o_ref):
    o_ref[...] = _ln(alpha * h_ref[...] + f_ref[...], g_ref[...], b_ref[...])


def _ln2(h2d, ffn2d, alpha, g, b):
    t, d = h2d.shape
    tm = LN_TM
    return pl.pallas_call(
        functools.partial(_ln2_kernel, alpha),
        grid=(t // tm,),
        in_specs=[
            pl.BlockSpec((tm, d), lambda i: (i, 0)),
            pl.BlockSpec((tm, d), lambda i: (i, 0)),
            pl.BlockSpec((1, d), lambda i: (0, 0)),
            pl.BlockSpec((1, d), lambda i: (0, 0)),
        ],
        out_specs=pl.BlockSpec((tm, d), lambda i: (i, 0)),
        out_shape=jax.ShapeDtypeStruct((t, d), F32),
        compiler_params=pltpu.CompilerParams(dimension_semantics=("arbitrary",)),
        name="ln2",
    )(h2d, ffn2d, g, b)


def kernel(x, in_ln_g, in_ln_b, w_in, b_in, pool_w, pool_scale, sgu_ln_g, sgu_ln_b, sgu_w, sgu_b, p_a, p_b,
           w_out, ln1_g, ln1_b, w_router, w_gate, w_up, w_down, ln2_g, ln2_b):
    n_batch, seq, d = x.shape
    depth = w_in.shape[0]
    n_e = w_router.shape[-1]
    cap = CAPACITY_FACTOR * seq // n_e
    alpha = (2 * depth) ** 0.25
    t = n_batch * seq
    hd = sgu_ln_g.shape[-1] // SGU_HEADS

    def row(a):
        return a.reshape(1, -1)

    wg_all, wu_all, wd_all = w_gate.astype(BF16), w_up.astype(BF16), w_down.astype(BF16)
    h = x.reshape(t, d)
    for l in range(depth):
        sgu_bias = jnp.repeat(sgu_b[l].T, hd, axis=1)
        h1, logits_t = _mixer(
            h, seq, alpha, l == 0, row(in_ln_g), row(in_ln_b),
            w_in[l].astype(BF16), row(b_in[l]), pool_w[l].astype(BF16), row(pool_scale[l]),
            row(sgu_ln_g[l]), row(sgu_ln_b[l]), sgu_w[l].astype(BF16), sgu_bias,
            p_a[l].astype(BF16), p_b[l].astype(BF16), w_out[l].astype(BF16),
            row(ln1_g[l]), row(ln1_b[l]), w_router[l].T.astype(BF16))
        idx, gates = _route(logits_t, n_batch, seq, cap)
        ffn = _moe(h1.reshape(n_batch, seq * d // LANES, LANES), idx, gates, l, wg_all, wu_all, wd_all)
        h = _ln2(h1, ffn.reshape(t, d), alpha, row(ln2_g[l]), row(ln2_b[l]))
    return h.reshape(n_batch, seq, d)
```

```python
import functools

import jax
import jax.numpy as jnp
from jax import lax
from jax.experimental import pallas as pl
from jax.experimental.pallas import tpu as pltpu

F32 = jnp.float32
BF16 = jnp.bfloat16

POOL_WINDOWS = (2, 4, 8, 16)
POOL_HALO = 8
SGU_CHUNK = 128
SGU_HEADS = 8
CAPACITY_FACTOR = 2
LN_EPS = 1e-5
LANES = 128
SUBLANES = 8
BF16_ROWS = 16
VMEM_LIMIT = 60 * 1024 * 1024

MIX_TM = 512
MOE_FC = 1024
LN_TM = 512


def _ln(x, g, b):
    mu = jnp.mean(x, axis=-1, keepdims=True)
    xc = x - mu
    var = jnp.mean(xc * xc, axis=-1, keepdims=True)
    return xc * lax.rsqrt(var + LN_EPS) * g + b


def _dot(a, b):
    return jnp.dot(a, b, preferred_element_type=F32)


def _slab_rows(ref, n, rpt):
    return jnp.concatenate([ref[pl.ds(q, n, stride=rpt), :] for q in range(rpt)], axis=1)


def _mixer_kernel(has_ffn, seq, alpha, *refs):
    if has_ffn:
        x_ref, xp_ref, xn_ref, f_ref, fp_ref, fn_ref = refs[:6]
        refs = refs[6:]
    else:
        x_ref, xp_ref, xn_ref = refs[:3]
        refs = refs[3:]
    (preg_ref, preb_ref, win_ref, bin_ref, poolw_ref, pscale_ref, sg_ref, sb_ref, sw_ref, sbias_ref,
     pa_ref, pb_ref, wout_ref, l1g_ref, l1b_ref, wrt_ref,
     h_ref, hs_ref, logit_ref,
     proj_ref, aext_ref, ya_ref, yb_ref) = refs
    tm, d = x_ref.shape
    rpt = d // LANES
    pw = ya_ref.shape[1]
    gd = pw // len(POOL_WINDOWS)
    sw = yb_ref.shape[1]
    hd = sw // SGU_HEADS
    o_u, o_v, o_ga, o_gb = pw, pw + sw, pw + 2 * sw, pw + 2 * sw + d

    i = pl.program_id(0)
    tiles_per_seq = seq // tm
    pos = i % tiles_per_seq

    x = x_ref[...]
    xp = xp_ref[...]
    xn = xn_ref[...]
    if has_ffn:
        x = alpha * x + _slab_rows(f_ref, tm, rpt)
        xp = alpha * xp + _slab_rows(fp_ref, POOL_HALO, rpt)
        xn = alpha * xn + _slab_rows(fn_ref, POOL_HALO, rpt)
    x = _ln(x, preg_ref[...], preb_ref[...])
    xp = _ln(xp, preg_ref[...], preb_ref[...])
    xn = _ln(xn, preg_ref[...], preb_ref[...])
    xb = x.astype(BF16)

    proj_ref[...] = _dot(xb, win_ref[...]) + bin_ref[...]
    ap = _dot(xp.astype(BF16), win_ref[:, 0:pw]) + bin_ref[:, 0:pw]
    an = _dot(xn.astype(BF16), win_ref[:, 0:pw]) + bin_ref[:, 0:pw]
    ap = jnp.where(pos == 0, 0.0, ap)
    an = jnp.where(pos == tiles_per_seq - 1, 0.0, an)
    aext_ref[0:POOL_HALO, :] = ap
    aext_ref[POOL_HALO:POOL_HALO + tm, :] = proj_ref[:, 0:pw]
    aext_ref[POOL_HALO + tm:2 * POOL_HALO + tm, :] = an

    for c in range(tm // SGU_CHUNK):
        r0 = c * SGU_CHUNK
        rows = pl.ds(r0, SGU_CHUNK)
        s = pos * tm + r0 + lax.broadcasted_iota(jnp.int32, (SGU_CHUNK, 1), 0)
        for g, w in enumerate(POOL_WINDOWS):
            cols = slice(g * gd, (g + 1) * gd)
            acc = aext_ref[pl.ds(POOL_HALO + r0 - w // 2, SGU_CHUNK), cols]
            for o in range(-w // 2 + 1, w // 2):
                acc = acc + aext_ref[pl.ds(POOL_HALO + r0 + o, SGU_CHUNK), cols]
            cnt = (jnp.minimum(s + w // 2, seq) - jnp.maximum(s - w // 2, 0)).astype(F32)
            pooled = acc / cnt - proj_ref[rows, cols]
            ya = _dot(pooled.astype(BF16), poolw_ref[g]) * pscale_ref[:, cols]
            ya_ref[rows, cols] = ya.astype(BF16)
        gu = jax.nn.gelu(proj_ref[rows, o_u:o_v])
        gv = jax.nn.gelu(proj_ref[rows, o_v:o_ga])
        vb = _ln(gv, sg_ref[...], sb_ref[...]).astype(BF16)
        for hh in range(SGU_HEADS):
            hc = slice(hh * hd, (hh + 1) * hd)
            mixed = _dot(sw_ref[hh], vb[:, hc]) + sbias_ref[:, hc]
            yb_ref[rows, hc] = (gu[:, hc] * mixed).astype(BF16)

    ta = _dot(ya_ref[...], pa_ref[...])
    tb = _dot(yb_ref[...], pb_ref[...])
    merged = (jax.nn.sigmoid(proj_ref[:, o_ga:o_gb]) * ta
              + jax.nn.sigmoid(proj_ref[:, o_gb:o_gb + d]) * tb)
    mix = _dot(merged.astype(BF16), wout_ref[...])
    h1 = _ln(alpha * x + mix, l1g_ref[...], l1b_ref[...])
    h_ref[...] = h1
    hb = h1.astype(BF16)
    logit_ref[...] = lax.dot_general(wrt_ref[...], hb, (((1,), (1,)), ((), ())), preferred_element_type=F32)

    for q in range(rpt):
        hs_ref[pl.ds(q, tm, stride=rpt), :] = h1[:, q * LANES:(q + 1) * LANES]


def _mixer(x2d, ffn_slab, seq, alpha, pre_g, pre_b, w_in, b_in, pool_w, pool_scale, sgu_g, sgu_b, sgu_w,
           sgu_bias, p_a, p_b, w_out, ln1_g, ln1_b, w_rt):
    t, d = x2d.shape
    tm = MIX_TM
    rpt = d // LANES
    n_e = w_rt.shape[0]
    pw = p_a.shape[0]
    sw = p_b.shape[0]
    hb = tm // POOL_HALO
    n_hb = t // POOL_HALO
    has_ffn = ffn_slab is not None

    def full(a):
        nd = a.ndim
        return pl.BlockSpec(a.shape, lambda i, _n=nd: (0,) * _n)

    def prev_halo(i):
        return (jnp.maximum(i * hb - 1, 0), 0)

    def next_halo(i):
        return (jnp.minimum((i + 1) * hb, n_hb - 1), 0)

    acts = [x2d, x2d, x2d]
    act_specs = [
        pl.BlockSpec((tm, d), lambda i: (i, 0)),
        pl.BlockSpec((POOL_HALO, d), prev_halo),
        pl.BlockSpec((POOL_HALO, d), next_halo),
    ]
    if has_ffn:
        acts += [ffn_slab, ffn_slab, ffn_slab]
        act_specs += [
            pl.BlockSpec((tm * rpt, LANES), lambda i: (i, 0)),
            pl.BlockSpec((POOL_HALO * rpt, LANES), prev_halo),
            pl.BlockSpec((POOL_HALO * rpt, LANES), next_halo),
        ]
    weights = (pre_g, pre_b, w_in, b_in, pool_w, pool_scale, sgu_g, sgu_b, sgu_w, sgu_bias, p_a, p_b, w_out,
               ln1_g, ln1_b, w_rt)
    return pl.pallas_call(
        functools.partial(_mixer_kernel, has_ffn, seq, alpha),
        grid=(t // tm,),
        in_specs=act_specs + [full(a) for a in weights],
        out_specs=[
            pl.BlockSpec((tm, d), lambda i: (i, 0)),
            pl.BlockSpec((tm * rpt, LANES), lambda i: (i, 0)),
            pl.BlockSpec((n_e, tm), lambda i: (0, i)),
        ],
        out_shape=[
            jax.ShapeDtypeStruct((t, d), F32),
            jax.ShapeDtypeStruct((t * rpt, LANES), F32),
            jax.ShapeDtypeStruct((n_e, t), F32),
        ],
        scratch_shapes=[
            pltpu.VMEM((tm, w_in.shape[1]), F32),
            pltpu.VMEM((tm + 2 * POOL_HALO, pw), F32),
            pltpu.VMEM((tm, pw), BF16),
            pltpu.VMEM((tm, sw), BF16),
        ],
        compiler_params=pltpu.CompilerParams(
            dimension_semantics=("arbitrary",), vmem_limit_bytes=VMEM_LIMIT),
        name="mixer",
    )(*acts, *weights)


def _route_kernel(cap, logit_ref, idx_ref, gate_ref, rank_ref, val_ref, hot_ref, res_ref):
    n_e, seq = logit_ref.shape
    n_tiles = seq // LANES
    tok_shift = 6
    chunk_tiles = 4

    lg = logit_ref[...]
    ex = jnp.exp(lg - jnp.max(lg, axis=0, keepdims=True))
    aff = ex / jnp.sum(ex, axis=0, keepdims=True)

    def bit_step(k, thr):
        cand = thr | jnp.left_shift(jnp.int32(1), 30 - k)
        cand_f = pltpu.bitcast(cand, F32)
        cnt = jnp.sum(jnp.where(aff >= cand_f, 1.0, 0.0), axis=1, keepdims=True)
        return jnp.where(cnt >= cap, cand, thr)

    thr = lax.fori_loop(0, 31, bit_step, jnp.zeros((n_e, 1), jnp.int32))
    thr_f = pltpu.bitcast(thr, F32)
    gt = aff > thr_f
    eq = aff == thr_f
    need = cap - jnp.sum(jnp.where(gt, 1.0, 0.0), axis=1, keepdims=True)

    tri = (lax.broadcasted_iota(jnp.int32, (LANES, LANES), 0)
           <= lax.broadcasted_iota(jnp.int32, (LANES, LANES), 1)).astype(BF16)
    carry_gt = jnp.zeros((n_e, 1), F32)
    carry_eq = jnp.zeros((n_e, 1), F32)
    for k in range(n_tiles):
        cols = slice(k * LANES, (k + 1) * LANES)
        gt_k = gt[:, cols]
        eq_k = eq[:, cols]
        c_gt = _dot(jnp.where(gt_k, 1.0, 0.0).astype(BF16), tri) + carry_gt
        c_eq = _dot(jnp.where(eq_k, 1.0, 0.0).astype(BF16), tri) + carry_eq
        carry_gt = c_gt[:, LANES - 1:LANES]
        carry_eq = c_eq[:, LANES - 1:LANES]
        sel_k = gt_k | (eq_k & (c_eq <= need))
        rank = c_gt + jnp.minimum(c_eq, need)
        rank_ref[k] = jnp.where(sel_k, rank, 0.0)

    hi = aff.astype(BF16)
    rest = aff - hi.astype(F32)
    mid = rest.astype(BF16)
    lo = (rest - mid.astype(F32)).astype(BF16)
    tok = lax.broadcasted_iota(jnp.int32, (BF16_ROWS, seq), 1)
    r_hi, r_lo = 3 * n_e, 3 * n_e + BF16_ROWS
    val_ref[0 * n_e:1 * n_e, :] = hi
    val_ref[1 * n_e:2 * n_e, :] = mid
    val_ref[2 * n_e:3 * n_e, :] = lo
    val_ref[r_hi:r_hi + BF16_ROWS, :] = lax.shift_right_logical(tok, tok_shift).astype(F32).astype(BF16)
    val_ref[r_lo:r_lo + BF16_ROWS, :] = (tok & ((1 << tok_shift) - 1)).astype(F32).astype(BF16)

    slot = (lax.broadcasted_iota(jnp.int32, (cap, LANES), 0) + 1).astype(F32)

    def per_expert(e, _):
        acc = jnp.zeros(res_ref.shape, F32)
        for c in range(n_tiles // chunk_tiles):
            for k in range(c * chunk_tiles, (c + 1) * chunk_tiles):
                hit = rank_ref[k, pl.ds(e, 1), :] == slot
                hot_ref[:, k * LANES:(k + 1) * LANES] = jnp.where(hit, 1.0, 0.0).astype(BF16)
            cols = slice(c * chunk_tiles * LANES, (c + 1) * chunk_tiles * LANES)
            acc = acc + lax.dot_general(val_ref[:, cols], hot_ref[:, cols], (((1,), (1,)), ((), ())),
                                        preferred_element_type=F32)
        res_ref[...] = acc
        tok_row = res_ref[r_hi:r_hi + 1, :] * float(1 << tok_shift) + res_ref[r_lo:r_lo + 1, :]
        gate_row = (res_ref[pl.ds(e, 1), :] + res_ref[pl.ds(n_e + e, 1), :]) + res_ref[pl.ds(2 * n_e + e, 1), :]
        idx_ref[0, pl.ds(e, 1), :] = tok_row.astype(jnp.int32)
        gate_ref[0, pl.ds(e, 1), :] = gate_row
        return 0

    lax.fori_loop(0, n_e, per_expert, 0)


def _route(logits_t, n_batch, seq, cap):
    n_e = logits_t.shape[0]
    return pl.pallas_call(
        functools.partial(_route_kernel, cap),
        grid=(n_batch,),
        in_specs=[pl.BlockSpec((n_e, seq), lambda b: (0, b))],
        out_specs=[
            pl.BlockSpec((1, n_e, cap), lambda b: (b, 0, 0)),
            pl.BlockSpec((1, n_e, cap), lambda b: (b, 0, 0)),
        ],
        out_shape=[
            jax.ShapeDtypeStruct((n_batch, n_e, cap), jnp.int32),
            jax.ShapeDtypeStruct((n_batch, n_e, cap), F32),
        ],
        scratch_shapes=[pltpu.VMEM((seq // LANES, n_e, LANES), F32),
                        pltpu.VMEM((3 * n_e + 2 * BF16_ROWS, seq), BF16),
                        pltpu.VMEM((cap, seq), BF16),
                        pltpu.VMEM((3 * n_e + 2 * BF16_ROWS, cap), F32)],
        compiler_params=pltpu.CompilerParams(
            dimension_semantics=("arbitrary",), vmem_limit_bytes=VMEM_LIMIT),
        name="route",
    )(logits_t)


def _moe_kernel(row_stride, idx_ref, gate_ref, hs_ref, wg_ref, wu_ref, wd_ref, out_ref, xe_ref, yt_ref):
    cap, d = xe_ref.shape
    rpt = d // LANES
    xt_ref = yt_ref
    e = pl.program_id(1)
    f = pl.program_id(2)
    n_f = pl.num_programs(2)
    unroll = 8

    @pl.when((e == 0) & (f == 0))
    def _():
        out_ref[...] = jnp.zeros_like(out_ref)

    @pl.when(f == 0)
    def _():
        def gather(jj, _):
            for u in range(unroll):
                j = jj * unroll + u
                r = pl.multiple_of(idx_ref[0, 0, j] * rpt, rpt)
                xt_ref[pl.ds(j, rpt, stride=row_stride), :] = hs_ref[0, pl.ds(r, rpt), :]
            return 0

        lax.fori_loop(0, cap // unroll, gather, 0)
        for q in range(rpt):
            xe_ref[:, q * LANES:(q + 1) * LANES] = xt_ref[pl.ds(q * row_stride, cap), :].astype(BF16)

    x = xe_ref[...]
    g = _dot(x, wg_ref[0, 0])
    u = _dot(x, wu_ref[0, 0])
    hid = (jax.nn.silu(g) * u).astype(BF16)
    y = _dot(hid, wd_ref[0, 0])

    @pl.when(f == 0)
    def _():
        for q in range(rpt):
            yt_ref[pl.ds(q * row_stride, cap), :] = y[:, q * LANES:(q + 1) * LANES]

    @pl.when(f > 0)
    def _():
        for q in range(rpt):
            yt_ref[pl.ds(q * row_stride, cap), :] += y[:, q * LANES:(q + 1) * LANES]

    @pl.when(f == n_f - 1)
    def _():
        def scatter(jj, _):
            rows = []
            vals = []
            for u in range(unroll):
                j = jj * unroll + u
                r = pl.multiple_of(idx_ref[0, 0, j] * rpt, rpt)
                slab = yt_ref[pl.ds(j, rpt, stride=row_stride), :] * gate_ref[0, 0, j]
                rows.append(r)
                vals.append(out_ref[0, pl.ds(r, rpt), :] + slab)
            for r, v in zip(rows, vals):
                out_ref[0, pl.ds(r, rpt), :] = v
            return 0

        lax.fori_loop(0, cap // unroll, scatter, 0)


def _moe(hs3, idx, gates, layer, w_gate, w_up, w_down):
    n_batch, srows, _ = hs3.shape
    _, n_e, d, d_exp = w_gate.shape
    cap = idx.shape[-1]
    fc = MOE_FC
    rpt = d // LANES
    row_stride = cap + SUBLANES
    idx3 = idx.reshape(n_batch * n_e, 1, cap)
    gates3 = gates.reshape(n_batch * n_e, 1, cap)
    return pl.pallas_call(
        functools.partial(_moe_kernel, row_stride),
        grid=(n_batch, n_e, d_exp // fc),
        in_specs=[
            pl.BlockSpec((1, 1, cap), lambda b, e, f: (b * n_e + e, 0, 0), memory_space=pltpu.SMEM),
            pl.BlockSpec((1, 1, cap), lambda b, e, f: (b * n_e + e, 0, 0), memory_space=pltpu.SMEM),
            pl.BlockSpec((1, srows, LANES), lambda b, e, f: (b, 0, 0), pipeline_mode=pl.Buffered(1)),
            pl.BlockSpec((1, 1, d, fc), lambda b, e, f: (layer, e, 0, f)),
            pl.BlockSpec((1, 1, d, fc), lambda b, e, f: (layer, e, 0, f)),
            pl.BlockSpec((1, 1, fc, d), lambda b, e, f: (layer, e, f, 0)),
        ],
        out_specs=pl.BlockSpec((1, srows, LANES), lambda b, e, f: (b, 0, 0), pipeline_mode=pl.Buffered(1)),
        out_shape=jax.ShapeDtypeStruct((n_batch, srows, LANES), F32),
        scratch_shapes=[
            pltpu.VMEM((cap, d), BF16),
            pltpu.VMEM((rpt * row_stride, LANES), F32),
        ],
        compiler_params=pltpu.CompilerParams(
            dimension_semantics=("arbitrary", "arbitrary", "arbitrary"), vmem_limit_bytes=VMEM_LIMIT),
        name="moe",
    )(idx3, gates3, hs3, w_gate, w_up, w_down)


def _ln2_kernel(alpha, h_ref, f_ref, g_ref, b_ref, o_ref):
    tm, d = h_ref.shape
    o_ref[...] = _ln(alpha * h_ref[...] + _slab_rows(f_ref, tm, d // LANES), g_ref[...], b_ref[...])


def _ln2(h2d, ffn_slab, alpha, g, b):
    t, d = h2d.shape
    tm = LN_TM
    rpt = d // LANES
    return pl.pallas_call(
        functools.partial(_ln2_kernel, alpha),
        grid=(t // tm,),
        in_specs=[
            pl.BlockSpec((tm, d), lambda i: (i, 0)),
            pl.BlockSpec((tm * rpt, LANES), lambda i: (i, 0)),
            pl.BlockSpec((1, d), lambda i: (0, 0)),
            pl.BlockSpec((1, d), lambda i: (0, 0)),
        ],
        out_specs=pl.BlockSpec((tm, d), lambda i: (i, 0)),
        out_shape=jax.ShapeDtypeStruct((t, d), F32),
        compiler_params=pltpu.CompilerParams(dimension_semantics=("arbitrary",)),
        name="ln2",
    )(h2d, ffn_slab, g, b)


def kernel(x, in_ln_g, in_ln_b, w_in, b_in, pool_w, pool_scale, sgu_ln_g, sgu_ln_b, sgu_w, sgu_b, p_a, p_b,
           w_out, ln1_g, ln1_b, w_router, w_gate, w_up, w_down, ln2_g, ln2_b):
    n_batch, seq, d = x.shape
    depth = w_in.shape[0]
    n_e = w_router.shape[-1]
    cap = CAPACITY_FACTOR * seq // n_e
    alpha = (2 * depth) ** 0.25
    t = n_batch * seq
    hd = sgu_ln_g.shape[-1] // SGU_HEADS
    rpt = d // LANES

    def row(a):
        return a.reshape(1, -1)

    wg_all, wu_all, wd_all = w_gate.astype(BF16), w_up.astype(BF16), w_down.astype(BF16)
    h = x.reshape(t, d)
    ffn = None
    pre_g, pre_b = in_ln_g, in_ln_b
    for l in range(depth):
        sgu_bias = jnp.repeat(sgu_b[l].T, hd, axis=1)
        h, hs, logits_t = _mixer(
            h, ffn, seq, alpha, row(pre_g), row(pre_b),
            w_in[l].astype(BF16), row(b_in[l]), pool_w[l].astype(BF16), row(pool_scale[l]),
            row(sgu_ln_g[l]), row(sgu_ln_b[l]), sgu_w[l].astype(BF16), sgu_bias,
            p_a[l].astype(BF16), p_b[l].astype(BF16), w_out[l].astype(BF16),
            row(ln1_g[l]), row(ln1_b[l]), w_router[l].T.astype(BF16))
        idx, gates = _route(logits_t, n_batch, seq, cap)
        ffn = _moe(hs.reshape(n_batch, seq * rpt, LANES), idx, gates, l, wg_all, wu_all, wd_all)
        ffn = ffn.reshape(t * rpt, LANES)
        pre_g, pre_b = ln2_g[l], ln2_b[l]
    out = _ln2(h, ffn, alpha, row(pre_g), row(pre_b))
    return out.reshape(n_batch, seq, d)
```

```python
import functools

import jax
import jax.numpy as jnp
from jax import lax
from jax.experimental import pallas as pl
from jax.experimental.pallas import tpu as pltpu

F32 = jnp.float32
BF16 = jnp.bfloat16

POOL_WINDOWS = (2, 4, 8, 16)
POOL_HALO = 8
SGU_CHUNK = 128
SGU_HEADS = 8
CAPACITY_FACTOR = 2
LN_EPS = 1e-5
LANES = 128
SUBLANES = 8
BF16_ROWS = 16
VMEM_LIMIT = 60 * 1024 * 1024

MIX_TM = 512
LN_TM = 512


def _ln(x, g, b):
    mu = jnp.mean(x, axis=-1, keepdims=True)
    xc = x - mu
    var = jnp.mean(xc * xc, axis=-1, keepdims=True)
    return xc * lax.rsqrt(var + LN_EPS) * g + b


def _dot(a, b):
    return jnp.dot(a, b, preferred_element_type=F32)


def _slab_rows(ref, n, rpt):
    return jnp.concatenate([ref[pl.ds(q, n, stride=rpt), :] for q in range(rpt)], axis=1)


def _mixer_kernel(has_ffn, seq, alpha, *refs):
    if has_ffn:
        x_ref, xp_ref, xn_ref, f_ref, fp_ref, fn_ref = refs[:6]
        refs = refs[6:]
    else:
        x_ref, xp_ref, xn_ref = refs[:3]
        refs = refs[3:]
    (preg_ref, preb_ref, win_ref, bin_ref, poolw_ref, pscale_ref, sg_ref, sb_ref, sw_ref, sbias_ref,
     pa_ref, pb_ref, wout_ref, l1g_ref, l1b_ref, wrt_ref,
     h_ref, hs_ref, logit_ref,
     proj_ref, aext_ref, ya_ref, yb_ref) = refs
    tm, d = x_ref.shape
    rpt = d // LANES
    pw = ya_ref.shape[1]
    gd = pw // len(POOL_WINDOWS)
    sw = yb_ref.shape[1]
    hd = sw // SGU_HEADS
    o_u, o_v, o_ga, o_gb = pw, pw + sw, pw + 2 * sw, pw + 2 * sw + d

    i = pl.program_id(0)
    tiles_per_seq = seq // tm
    pos = i % tiles_per_seq

    x = x_ref[...]
    xp = xp_ref[...]
    xn = xn_ref[...]
    if has_ffn:
        x = alpha * x + _slab_rows(f_ref, tm, rpt)
        xp = alpha * xp + _slab_rows(fp_ref, POOL_HALO, rpt)
        xn = alpha * xn + _slab_rows(fn_ref, POOL_HALO, rpt)
    x = _ln(x, preg_ref[...], preb_ref[...])
    xp = _ln(xp, preg_ref[...], preb_ref[...])
    xn = _ln(xn, preg_ref[...], preb_ref[...])
    xb = x.astype(BF16)

    proj_ref[...] = _dot(xb, win_ref[...]) + bin_ref[...]
    ap = _dot(xp.astype(BF16), win_ref[:, 0:pw]) + bin_ref[:, 0:pw]
    an = _dot(xn.astype(BF16), win_ref[:, 0:pw]) + bin_ref[:, 0:pw]
    ap = jnp.where(pos == 0, 0.0, ap)
    an = jnp.where(pos == tiles_per_seq - 1, 0.0, an)
    aext_ref[0:POOL_HALO, :] = ap
    aext_ref[POOL_HALO:POOL_HALO + tm, :] = proj_ref[:, 0:pw]
    aext_ref[POOL_HALO + tm:2 * POOL_HALO + tm, :] = an

    for c in range(tm // SGU_CHUNK):
        r0 = c * SGU_CHUNK
        rows = pl.ds(r0, SGU_CHUNK)
        s = pos * tm + r0 + lax.broadcasted_iota(jnp.int32, (SGU_CHUNK, 1), 0)
        for g, w in enumerate(POOL_WINDOWS):
            cols = slice(g * gd, (g + 1) * gd)
            acc = aext_ref[pl.ds(POOL_HALO + r0 - w // 2, SGU_CHUNK), cols]
            for o in range(-w // 2 + 1, w // 2):
                acc = acc + aext_ref[pl.ds(POOL_HALO + r0 + o, SGU_CHUNK), cols]
            cnt = (jnp.minimum(s + w // 2, seq) - jnp.maximum(s - w // 2, 0)).astype(F32)
            pooled = acc / cnt - proj_ref[rows, cols]
            ya = _dot(pooled.astype(BF16), poolw_ref[g]) * pscale_ref[:, cols]
            ya_ref[rows, cols] = ya.astype(BF16)
        gu = jax.nn.gelu(proj_ref[rows, o_u:o_v])
        gv = jax.nn.gelu(proj_ref[rows, o_v:o_ga])
        vb = _ln(gv, sg_ref[...], sb_ref[...]).astype(BF16)
        for hh in range(SGU_HEADS):
            hc = slice(hh * hd, (hh + 1) * hd)
            mixed = _dot(sw_ref[hh], vb[:, hc]) + sbias_ref[:, hc]
            yb_ref[rows, hc] = (gu[:, hc] * mixed).astype(BF16)

    ta = _dot(ya_ref[...], pa_ref[...])
    tb = _dot(yb_ref[...], pb_ref[...])
    merged = (jax.nn.sigmoid(proj_ref[:, o_ga:o_gb]) * ta
              + jax.nn.sigmoid(proj_ref[:, o_gb:o_gb + d]) * tb)
    mix = _dot(merged.astype(BF16), wout_ref[...])
    h1 = _ln(alpha * x + mix, l1g_ref[...], l1b_ref[...])
    h_ref[...] = h1
    hb = h1.astype(BF16)
    logit_ref[...] = lax.dot_general(wrt_ref[...], hb, (((1,), (1,)), ((), ())), preferred_element_type=F32)

    for q in range(rpt):
        hs_ref[pl.ds(q, tm, stride=rpt), :] = h1[:, q * LANES:(q + 1) * LANES]


def _mixer(x2d, ffn_slab, seq, alpha, pre_g, pre_b, w_in, b_in, pool_w, pool_scale, sgu_g, sgu_b, sgu_w,
           sgu_bias, p_a, p_b, w_out, ln1_g, ln1_b, w_rt):
    t, d = x2d.shape
    tm = MIX_TM
    rpt = d // LANES
    n_e = w_rt.shape[0]
    pw = p_a.shape[0]
    sw = p_b.shape[0]
    hb = tm // POOL_HALO
    n_hb = t // POOL_HALO
    has_ffn = ffn_slab is not None

    def full(a):
        nd = a.ndim
        return pl.BlockSpec(a.shape, lambda i, _n=nd: (0,) * _n)

    def prev_halo(i):
        return (jnp.maximum(i * hb - 1, 0), 0)

    def next_halo(i):
        return (jnp.minimum((i + 1) * hb, n_hb - 1), 0)

    acts = [x2d, x2d, x2d]
    act_specs = [
        pl.BlockSpec((tm, d), lambda i: (i, 0)),
        pl.BlockSpec((POOL_HALO, d), prev_halo),
        pl.BlockSpec((POOL_HALO, d), next_halo),
    ]
    if has_ffn:
        acts += [ffn_slab, ffn_slab, ffn_slab]
        act_specs += [
            pl.BlockSpec((tm * rpt, LANES), lambda i: (i, 0)),
            pl.BlockSpec((POOL_HALO * rpt, LANES), prev_halo),
            pl.BlockSpec((POOL_HALO * rpt, LANES), next_halo),
        ]
    weights = (pre_g, pre_b, w_in, b_in, pool_w, pool_scale, sgu_g, sgu_b, sgu_w, sgu_bias, p_a, p_b, w_out,
               ln1_g, ln1_b, w_rt)
    return pl.pallas_call(
        functools.partial(_mixer_kernel, has_ffn, seq, alpha),
        grid=(t // tm,),
        in_specs=act_specs + [full(a) for a in weights],
        out_specs=[
            pl.BlockSpec((tm, d), lambda i: (i, 0)),
            pl.BlockSpec((tm * rpt, LANES), lambda i: (i, 0)),
            pl.BlockSpec((n_e, tm), lambda i: (0, i)),
        ],
        out_shape=[
            jax.ShapeDtypeStruct((t, d), F32),
            jax.ShapeDtypeStruct((t * rpt, LANES), F32),
            jax.ShapeDtypeStruct((n_e, t), F32),
        ],
        scratch_shapes=[
            pltpu.VMEM((tm, w_in.shape[1]), F32),
            pltpu.VMEM((tm + 2 * POOL_HALO, pw), F32),
            pltpu.VMEM((tm, pw), BF16),
            pltpu.VMEM((tm, sw), BF16),
        ],
        compiler_params=pltpu.CompilerParams(
            dimension_semantics=("arbitrary",), vmem_limit_bytes=VMEM_LIMIT),
        name="mixer",
    )(*acts, *weights)


def _route_kernel(cap, logit_ref, idx_ref, gate_ref, rank_ref, val_ref, hot_ref, res_ref):
    n_e, seq = logit_ref.shape
    n_tiles = seq // LANES
    tok_shift = 6
    chunk_tiles = 4

    lg = logit_ref[...]
    ex = jnp.exp(lg - jnp.max(lg, axis=0, keepdims=True))
    aff = ex / jnp.sum(ex, axis=0, keepdims=True)

    def bit_step(k, thr):
        cand = thr | jnp.left_shift(jnp.int32(1), 30 - k)
        cand_f = pltpu.bitcast(cand, F32)
        cnt = jnp.sum(jnp.where(aff >= cand_f, 1.0, 0.0), axis=1, keepdims=True)
        return jnp.where(cnt >= cap, cand, thr)

    thr = lax.fori_loop(0, 31, bit_step, jnp.zeros((n_e, 1), jnp.int32))
    thr_f = pltpu.bitcast(thr, F32)
    gt = aff > thr_f
    eq = aff == thr_f
    need = cap - jnp.sum(jnp.where(gt, 1.0, 0.0), axis=1, keepdims=True)

    tri = (lax.broadcasted_iota(jnp.int32, (LANES, LANES), 0)
           <= lax.broadcasted_iota(jnp.int32, (LANES, LANES), 1)).astype(BF16)
    carry_gt = jnp.zeros((n_e, 1), F32)
    carry_eq = jnp.zeros((n_e, 1), F32)
    for k in range(n_tiles):
        cols = slice(k * LANES, (k + 1) * LANES)
        gt_k = gt[:, cols]
        eq_k = eq[:, cols]
        c_gt = _dot(jnp.where(gt_k, 1.0, 0.0).astype(BF16), tri) + carry_gt
        c_eq = _dot(jnp.where(eq_k, 1.0, 0.0).astype(BF16), tri) + carry_eq
        carry_gt = c_gt[:, LANES - 1:LANES]
        carry_eq = c_eq[:, LANES - 1:LANES]
        sel_k = gt_k | (eq_k & (c_eq <= need))
        rank = c_gt + jnp.minimum(c_eq, need)
        rank_ref[k] = jnp.where(sel_k, rank, 0.0)

    hi = aff.astype(BF16)
    rest = aff - hi.astype(F32)
    mid = rest.astype(BF16)
    lo = (rest - mid.astype(F32)).astype(BF16)
    tok = lax.broadcasted_iota(jnp.int32, (BF16_ROWS, seq), 1)
    r_hi, r_lo = 3 * n_e, 3 * n_e + BF16_ROWS
    val_ref[0 * n_e:1 * n_e, :] = hi
    val_ref[1 * n_e:2 * n_e, :] = mid
    val_ref[2 * n_e:3 * n_e, :] = lo
    val_ref[r_hi:r_hi + BF16_ROWS, :] = lax.shift_right_logical(tok, tok_shift).astype(F32).astype(BF16)
    val_ref[r_lo:r_lo + BF16_ROWS, :] = (tok & ((1 << tok_shift) - 1)).astype(F32).astype(BF16)

    slot = (lax.broadcasted_iota(jnp.int32, (cap, LANES), 0) + 1).astype(F32)

    def per_expert(e, _):
        acc = jnp.zeros(res_ref.shape, F32)
        for c in range(n_tiles // chunk_tiles):
            for k in range(c * chunk_tiles, (c + 1) * chunk_tiles):
                hit = rank_ref[k, pl.ds(e, 1), :] == slot
                hot_ref[:, k * LANES:(k + 1) * LANES] = jnp.where(hit, 1.0, 0.0).astype(BF16)
            cols = slice(c * chunk_tiles * LANES, (c + 1) * chunk_tiles * LANES)
            acc = acc + lax.dot_general(val_ref[:, cols], hot_ref[:, cols], (((1,), (1,)), ((), ())),
                                        preferred_element_type=F32)
        res_ref[...] = acc
        tok_row = res_ref[r_hi:r_hi + 1, :] * float(1 << tok_shift) + res_ref[r_lo:r_lo + 1, :]
        gate_row = (res_ref[pl.ds(e, 1), :] + res_ref[pl.ds(n_e + e, 1), :]) + res_ref[pl.ds(2 * n_e + e, 1), :]
        idx_ref[0, pl.ds(e, 1), :] = tok_row.astype(jnp.int32)
        gate_ref[0, pl.ds(e, 1), :] = gate_row
        return 0

    lax.fori_loop(0, n_e, per_expert, 0)


def _route(logits_t, n_batch, seq, cap):
    n_e = logits_t.shape[0]
    return pl.pallas_call(
        functools.partial(_route_kernel, cap),
        grid=(n_batch,),
        in_specs=[pl.BlockSpec((n_e, seq), lambda b: (0, b))],
        out_specs=[
            pl.BlockSpec((1, n_e, cap), lambda b: (b, 0, 0)),
            pl.BlockSpec((1, n_e, cap), lambda b: (b, 0, 0)),
        ],
        out_shape=[
            jax.ShapeDtypeStruct((n_batch, n_e, cap), jnp.int32),
            jax.ShapeDtypeStruct((n_batch, n_e, cap), F32),
        ],
        scratch_shapes=[pltpu.VMEM((seq // LANES, n_e, LANES), F32),
                        pltpu.VMEM((3 * n_e + 2 * BF16_ROWS, seq), BF16),
                        pltpu.VMEM((cap, seq), BF16),
                        pltpu.VMEM((3 * n_e + 2 * BF16_ROWS, cap), F32)],
        compiler_params=pltpu.CompilerParams(
            dimension_semantics=("arbitrary",), vmem_limit_bytes=VMEM_LIMIT),
        name="route",
    )(logits_t)


def _moe_kernel(row_stride, idx_ref, idxp_ref, idxn_ref, gate_ref, gatep_ref, hs_ref, wg_ref, wu_ref, wd_ref,
                out_ref, xe_ref, xt_ref, yt_ref):
    cap, d = xe_ref.shape
    rpt = d // LANES
    e = pl.program_id(1)
    f = pl.program_id(2)
    n_e = pl.num_programs(1)
    group = 8

    def gather_row(src_idx_ref, j):
        r = pl.multiple_of(src_idx_ref[0, 0, j] * rpt, rpt)
        xt_ref[pl.ds(j, rpt, stride=row_stride), :] = hs_ref[0, pl.ds(r, rpt), :]

    def scatter_rows(src_idx_ref, src_gate_ref, js):
        rows = [pl.multiple_of(src_idx_ref[0, 0, j] * rpt, rpt) for j in js]
        vals = [out_ref[0, pl.ds(r, rpt), :] + yt_ref[pl.ds(j, rpt, stride=row_stride), :] * src_gate_ref[0, 0, j]
                for r, j in zip(rows, js)]
        for r, v in zip(rows, vals):
            out_ref[0, pl.ds(r, rpt), :] = v

    def ffn_half():
        x = xe_ref[...]
        g = _dot(x, wg_ref[0, 0])
        u = _dot(x, wu_ref[0, 0])
        hid = (jax.nn.silu(g) * u).astype(BF16)
        return _dot(hid, wd_ref[0, 0])

    @pl.when((e == 0) & (f == 0))
    def _():
        out_ref[...] = jnp.zeros_like(out_ref)
        yt_ref[...] = jnp.zeros_like(yt_ref)

        def gather(jj, _):
            for u in range(group):
                gather_row(idx_ref, jj * group + u)
            return 0

        lax.fori_loop(0, cap // group, gather, 0)

    @pl.when(f == 0)
    def _():
        for q in range(rpt):
            xe_ref[:, q * LANES:(q + 1) * LANES] = xt_ref[pl.ds(q * row_stride, cap), :].astype(BF16)
        for j0 in range(0, cap, group):
            scatter_rows(idxp_ref, gatep_ref, range(j0, j0 + group))
        y = ffn_half()
        for q in range(rpt):
            yt_ref[pl.ds(q * row_stride, cap), :] = y[:, q * LANES:(q + 1) * LANES]

    @pl.when(f == 1)
    def _():
        for j in range(cap):
            gather_row(idxn_ref, j)
        y = ffn_half()
        for q in range(rpt):
            yt_ref[pl.ds(q * row_stride, cap), :] += y[:, q * LANES:(q + 1) * LANES]

    @pl.when((e == n_e - 1) & (f == 1))
    def _():
        def scatter(jj, _):
            scatter_rows(idx_ref, gate_ref, [jj * group + u for u in range(group)])
            return 0

        lax.fori_loop(0, cap // group, scatter, 0)


def _moe(hs3, idx, gates, layer, w_gate, w_up, w_down):
    n_batch, srows, _ = hs3.shape
    _, n_e, d, d_exp = w_gate.shape
    cap = idx.shape[-1]
    fc = d_exp // 2
    rpt = d // LANES
    row_stride = cap + SUBLANES
    n_be = n_batch * n_e
    idx3 = idx.reshape(n_be, 1, cap)
    gates3 = jnp.concatenate([gates.reshape(n_be, 1, cap), jnp.zeros((1, 1, cap), F32)], axis=0)

    def smem(index_map):
        return pl.BlockSpec((1, 1, cap), index_map, memory_space=pltpu.SMEM)

    return pl.pallas_call(
        functools.partial(_moe_kernel, row_stride),
        grid=(n_batch, n_e, 2),
        in_specs=[
            smem(lambda b, e, f: (b * n_e + e, 0, 0)),
            smem(lambda b, e, f: (jnp.maximum(b * n_e + e - 1, 0), 0, 0)),
            smem(lambda b, e, f: (jnp.minimum(b * n_e + e + 1, n_be - 1), 0, 0)),
            smem(lambda b, e, f: (b * n_e + e, 0, 0)),
            smem(lambda b, e, f: (jnp.where(e == 0, n_be, b * n_e + e - 1), 0, 0)),
            pl.BlockSpec((1, srows, LANES), lambda b, e, f: (b, 0, 0), pipeline_mode=pl.Buffered(1)),
            pl.BlockSpec((1, 1, d, fc), lambda b, e, f: (layer, e, 0, f)),
            pl.BlockSpec((1, 1, d, fc), lambda b, e, f: (layer, e, 0, f)),
            pl.BlockSpec((1, 1, fc, d), lambda b, e, f: (layer, e, f, 0)),
        ],
        out_specs=pl.BlockSpec((1, srows, LANES), lambda b, e, f: (b, 0, 0), pipeline_mode=pl.Buffered(1)),
        out_shape=jax.ShapeDtypeStruct((n_batch, srows, LANES), F32),
        scratch_shapes=[
            pltpu.VMEM((cap, d), BF16),
            pltpu.VMEM((rpt * row_stride, LANES), F32),
            pltpu.VMEM((rpt * row_stride, LANES), F32),
        ],
        compiler_params=pltpu.CompilerParams(
            dimension_semantics=("arbitrary", "arbitrary", "arbitrary"), vmem_limit_bytes=VMEM_LIMIT),
        name="moe",
    )(idx3, idx3, idx3, gates3, gates3, hs3, w_gate, w_up, w_down)


def _ln2_kernel(alpha, h_ref, f_ref, g_ref, b_ref, o_ref):
    tm, d = h_ref.shape
    o_ref[...] = _ln(alpha * h_ref[...] + _slab_rows(f_ref, tm, d // LANES), g_ref[...], b_ref[...])


def _ln2(h2d, ffn_slab, alpha, g, b):
    t, d = h2d.shape
    tm = LN_TM
    rpt = d // LANES
    return pl.pallas_call(
        functools.partial(_ln2_kernel, alpha),
        grid=(t // tm,),
        in_specs=[
            pl.BlockSpec((tm, d), lambda i: (i, 0)),
            pl.BlockSpec((tm * rpt, LANES), lambda i: (i, 0)),
            pl.BlockSpec((1, d), lambda i: (0, 0)),
            pl.BlockSpec((1, d), lambda i: (0, 0)),
        ],
        out_specs=pl.BlockSpec((tm, d), lambda i: (i, 0)),
        out_shape=jax.ShapeDtypeStruct((t, d), F32),
        compiler_params=pltpu.CompilerParams(dimension_semantics=("arbitrary",)),
        name="ln2",
    )(h2d, ffn_slab, g, b)


def kernel(x, in_ln_g, in_ln_b, w_in, b_in, pool_w, pool_scale, sgu_ln_g, sgu_ln_b, sgu_w, sgu_b, p_a, p_b,
           w_out, ln1_g, ln1_b, w_router, w_gate, w_up, w_down, ln2_g, ln2_b):
    n_batch, seq, d = x.shape
    depth = w_in.shape[0]
    n_e = w_router.shape[-1]
    cap = CAPACITY_FACTOR * seq // n_e
    alpha = (2 * depth) ** 0.25
    t = n_batch * seq
    hd = sgu_ln_g.shape[-1] // SGU_HEADS
    rpt = d // LANES

    def row(a):
        return a.reshape(1, -1)

    wg_all, wu_all, wd_all = w_gate.astype(BF16), w_up.astype(BF16), w_down.astype(BF16)
    h = x.reshape(t, d)
    ffn = None
    pre_g, pre_b = in_ln_g, in_ln_b
    for l in range(depth):
        sgu_bias = jnp.repeat(sgu_b[l].T, hd, axis=1)
        h, hs, logits_t = _mixer(
            h, ffn, seq, alpha, row(pre_g), row(pre_b),
            w_in[l].astype(BF16), row(b_in[l]), pool_w[l].astype(BF16), row(pool_scale[l]),
            row(sgu_ln_g[l]), row(sgu_ln_b[l]), sgu_w[l].astype(BF16), sgu_bias,
            p_a[l].astype(BF16), p_b[l].astype(BF16), w_out[l].astype(BF16),
            row(ln1_g[l]), row(ln1_b[l]), w_router[l].T.astype(BF16))
        idx, gates = _route(logits_t, n_batch, seq, cap)
        ffn = _moe(hs.reshape(n_batch, seq * rpt, LANES), idx, gates, l, wg_all, wu_all, wd_all)
        ffn = ffn.reshape(t * rpt, LANES)
        pre_g, pre_b = ln2_g[l], ln2_b[l]
    out = _ln2(h, ffn, alpha, row(pre_g), row(pre_b))
    return out.reshape(n_batch, seq, d)
```

```python
import functools

import jax
import jax.numpy as jnp
from jax import lax
from jax.experimental import pallas as pl
from jax.experimental.pallas import tpu as pltpu

F32 = jnp.float32
BF16 = jnp.bfloat16

POOL_WINDOWS = (2, 4, 8, 16)
POOL_HALO = 8
SGU_CHUNK = 128
SGU_HEADS = 8
CAPACITY_FACTOR = 2
LN_EPS = 1e-5
LANES = 128
SUBLANES = 8
BF16_ROWS = 16
VMEM_LIMIT = 60 * 1024 * 1024

MIX_TM = 512
LN_TM = 512


def _ln(x, g, b):
    mu = jnp.mean(x, axis=-1, keepdims=True)
    xc = x - mu
    var = jnp.mean(xc * xc, axis=-1, keepdims=True)
    return xc * lax.rsqrt(var + LN_EPS) * g + b


def _dot(a, b):
    return jnp.dot(a, b, preferred_element_type=F32)


def _slab_rows(ref, n, rpt):
    return jnp.concatenate([ref[pl.ds(q, n, stride=rpt), :] for q in range(rpt)], axis=1)


def _cast_plan(stacked, layer, n_steps, step_of):
    ops, in_specs, out_specs, out_shapes = [], [], [], []
    for w in stacked:
        n_l, n_e, r, c = w.shape
        rows = n_e * r // n_steps
        ops.append(w.reshape(n_l * n_steps, rows, c))

        def in_map(*ids, _l=layer):
            return (_l * n_steps + step_of(*ids), 0, 0)

        def out_map(*ids):
            return (step_of(*ids), 0, 0)

        in_specs.append(pl.BlockSpec((1, rows, c), in_map))
        out_specs.append(pl.BlockSpec((1, rows, c), out_map))
        out_shapes.append(jax.ShapeDtypeStruct((n_steps, rows, c), BF16))
    return ops, in_specs, out_specs, out_shapes


def _cast_blocks(src_refs, dst_refs):
    for s, o in zip(src_refs, dst_refs):
        o[...] = s[...].astype(BF16)


def _mixer_kernel(has_ffn, n_cast, seq, alpha, *refs):
    if has_ffn:
        x_ref, xp_ref, xn_ref, f_ref, fp_ref, fn_ref = refs[:6]
        refs = refs[6:]
    else:
        x_ref, xp_ref, xn_ref = refs[:3]
        refs = refs[3:]
    cast_src, refs = refs[:n_cast], refs[n_cast:]
    (preg_ref, preb_ref, win_ref, bin_ref, poolw_ref, pscale_ref, sg_ref, sb_ref, sw_ref, sbias_ref,
     pa_ref, pb_ref, wout_ref, l1g_ref, l1b_ref, wrt_ref,
     h_ref, hs_ref, logit_ref) = refs[:19]
    cast_dst, refs = refs[19:19 + n_cast], refs[19 + n_cast:]
    proj_ref, aext_ref, ya_ref, yb_ref = refs
    _cast_blocks(cast_src, cast_dst)
    tm, d = x_ref.shape
    rpt = d // LANES
    pw = ya_ref.shape[1]
    gd = pw // len(POOL_WINDOWS)
    sw = yb_ref.shape[1]
    hd = sw // SGU_HEADS
    o_u, o_v, o_ga, o_gb = pw, pw + sw, pw + 2 * sw, pw + 2 * sw + d

    i = pl.program_id(0)
    tiles_per_seq = seq // tm
    pos = i % tiles_per_seq

    x = x_ref[...]
    xp = xp_ref[...]
    xn = xn_ref[...]
    if has_ffn:
        x = alpha * x + _slab_rows(f_ref, tm, rpt)
        xp = alpha * xp + _slab_rows(fp_ref, POOL_HALO, rpt)
        xn = alpha * xn + _slab_rows(fn_ref, POOL_HALO, rpt)
    x = _ln(x, preg_ref[...], preb_ref[...])
    xp = _ln(xp, preg_ref[...], preb_ref[...])
    xn = _ln(xn, preg_ref[...], preb_ref[...])
    xb = x.astype(BF16)

    proj_ref[...] = _dot(xb, win_ref[...]) + bin_ref[...]
    ap = _dot(xp.astype(BF16), win_ref[:, 0:pw]) + bin_ref[:, 0:pw]
    an = _dot(xn.astype(BF16), win_ref[:, 0:pw]) + bin_ref[:, 0:pw]
    ap = jnp.where(pos == 0, 0.0, ap)
    an = jnp.where(pos == tiles_per_seq - 1, 0.0, an)
    aext_ref[0:POOL_HALO, :] = ap
    aext_ref[POOL_HALO:POOL_HALO + tm, :] = proj_ref[:, 0:pw]
    aext_ref[POOL_HALO + tm:2 * POOL_HALO + tm, :] = an

    for c in range(tm // SGU_CHUNK):
        r0 = c * SGU_CHUNK
        rows = pl.ds(r0, SGU_CHUNK)
        s = pos * tm + r0 + lax.broadcasted_iota(jnp.int32, (SGU_CHUNK, 1), 0)
        for g, w in enumerate(POOL_WINDOWS):
            cols = slice(g * gd, (g + 1) * gd)
            acc = aext_ref[pl.ds(POOL_HALO + r0 - w // 2, SGU_CHUNK), cols]
            for o in range(-w // 2 + 1, w // 2):
                acc = acc + aext_ref[pl.ds(POOL_HALO + r0 + o, SGU_CHUNK), cols]
            cnt = (jnp.minimum(s + w // 2, seq) - jnp.maximum(s - w // 2, 0)).astype(F32)
            pooled = acc / cnt - proj_ref[rows, cols]
            ya = _dot(pooled.astype(BF16), poolw_ref[g]) * pscale_ref[:, cols]
            ya_ref[rows, cols] = ya.astype(BF16)
        gu = jax.nn.gelu(proj_ref[rows, o_u:o_v])
        gv = jax.nn.gelu(proj_ref[rows, o_v:o_ga])
        vb = _ln(gv, sg_ref[...], sb_ref[...]).astype(BF16)
        for hh in range(SGU_HEADS):
            hc = slice(hh * hd, (hh + 1) * hd)
            mixed = _dot(sw_ref[hh], vb[:, hc]) + sbias_ref[:, hc]
            yb_ref[rows, hc] = (gu[:, hc] * mixed).astype(BF16)

    ta = _dot(ya_ref[...], pa_ref[...])
    tb = _dot(yb_ref[...], pb_ref[...])
    merged = (jax.nn.sigmoid(proj_ref[:, o_ga:o_gb]) * ta
              + jax.nn.sigmoid(proj_ref[:, o_gb:o_gb + d]) * tb)
    mix = _dot(merged.astype(BF16), wout_ref[...])
    h1 = _ln(alpha * x + mix, l1g_ref[...], l1b_ref[...])
    h_ref[...] = h1
    hb = h1.astype(BF16)
    logit_ref[...] = lax.dot_general(wrt_ref[...], hb, (((1,), (1,)), ((), ())), preferred_element_type=F32)

    for q in range(rpt):
        hs_ref[pl.ds(q, tm, stride=rpt), :] = h1[:, q * LANES:(q + 1) * LANES]


def _mixer(x2d, ffn_slab, cast, seq, alpha, pre_g, pre_b, w_in, b_in, pool_w, pool_scale, sgu_g, sgu_b, sgu_w,
           sgu_bias, p_a, p_b, w_out, ln1_g, ln1_b, w_rt):
    t, d = x2d.shape
    tm = MIX_TM
    rpt = d // LANES
    n_e = w_rt.shape[0]
    pw = p_a.shape[0]
    sw = p_b.shape[0]
    hb = tm // POOL_HALO
    n_hb = t // POOL_HALO
    has_ffn = ffn_slab is not None

    def full(a):
        nd = a.ndim
        return pl.BlockSpec(a.shape, lambda i, _n=nd: (0,) * _n)

    def prev_halo(i):
        return (jnp.maximum(i * hb - 1, 0), 0)

    def next_halo(i):
        return (jnp.minimum((i + 1) * hb, n_hb - 1), 0)

    acts = [x2d, x2d, x2d]
    act_specs = [
        pl.BlockSpec((tm, d), lambda i: (i, 0)),
        pl.BlockSpec((POOL_HALO, d), prev_halo),
        pl.BlockSpec((POOL_HALO, d), next_halo),
    ]
    if has_ffn:
        acts += [ffn_slab, ffn_slab, ffn_slab]
        act_specs += [
            pl.BlockSpec((tm * rpt, LANES), lambda i: (i, 0)),
            pl.BlockSpec((POOL_HALO * rpt, LANES), prev_halo),
            pl.BlockSpec((POOL_HALO * rpt, LANES), next_halo),
        ]
    weights = (pre_g, pre_b, w_in, b_in, pool_w, pool_scale, sgu_g, sgu_b, sgu_w, sgu_bias, p_a, p_b, w_out,
               ln1_g, ln1_b, w_rt)
    c_ops, c_in, c_out, c_shapes = ([], [], [], []) if cast is None else _cast_plan(
        cast[0], cast[1], t // tm, lambda i: i)
    return pl.pallas_call(
        functools.partial(_mixer_kernel, has_ffn, len(c_ops), seq, alpha),
        grid=(t // tm,),
        in_specs=act_specs + c_in + [full(a) for a in weights],
        out_specs=[
            pl.BlockSpec((tm, d), lambda i: (i, 0)),
            pl.BlockSpec((tm * rpt, LANES), lambda i: (i, 0)),
            pl.BlockSpec((n_e, tm), lambda i: (0, i)),
        ] + c_out,
        out_shape=[
            jax.ShapeDtypeStruct((t, d), F32),
            jax.ShapeDtypeStruct((t * rpt, LANES), F32),
            jax.ShapeDtypeStruct((n_e, t), F32),
        ] + c_shapes,
        scratch_shapes=[
            pltpu.VMEM((tm, w_in.shape[1]), F32),
            pltpu.VMEM((tm + 2 * POOL_HALO, pw), F32),
            pltpu.VMEM((tm, pw), BF16),
            pltpu.VMEM((tm, sw), BF16),
        ],
        compiler_params=pltpu.CompilerParams(
            dimension_semantics=("arbitrary",), vmem_limit_bytes=VMEM_LIMIT),
        name="mixer",
    )(*acts, *c_ops, *weights)


def _route_kernel(cap, logit_ref, idx_ref, gate_ref, rank_ref, val_ref, hot_ref, res_ref):
    n_e, seq = logit_ref.shape
    n_tiles = seq // LANES
    tok_shift = 6
    chunk_tiles = 4

    lg = logit_ref[...]
    ex = jnp.exp(lg - jnp.max(lg, axis=0, keepdims=True))
    aff = ex / jnp.sum(ex, axis=0, keepdims=True)

    def bit_step(k, thr):
        cand = thr | jnp.left_shift(jnp.int32(1), 30 - k)
        cand_f = pltpu.bitcast(cand, F32)
        cnt = jnp.sum(jnp.where(aff >= cand_f, 1.0, 0.0), axis=1, keepdims=True)
        return jnp.where(cnt >= cap, cand, thr)

    thr = lax.fori_loop(0, 31, bit_step, jnp.zeros((n_e, 1), jnp.int32))
    thr_f = pltpu.bitcast(thr, F32)
    gt = aff > thr_f
    eq = aff == thr_f
    need = cap - jnp.sum(jnp.where(gt, 1.0, 0.0), axis=1, keepdims=True)

    tri = (lax.broadcasted_iota(jnp.int32, (LANES, LANES), 0)
           <= lax.broadcasted_iota(jnp.int32, (LANES, LANES), 1)).astype(BF16)
    carry_gt = jnp.zeros((n_e, 1), F32)
    carry_eq = jnp.zeros((n_e, 1), F32)
    for k in range(n_tiles):
        cols = slice(k * LANES, (k + 1) * LANES)
        gt_k = gt[:, cols]
        eq_k = eq[:, cols]
        c_gt = _dot(jnp.where(gt_k, 1.0, 0.0).astype(BF16), tri) + carry_gt
        c_eq = _dot(jnp.where(eq_k, 1.0, 0.0).astype(BF16), tri) + carry_eq
        carry_gt = c_gt[:, LANES - 1:LANES]
        carry_eq = c_eq[:, LANES - 1:LANES]
        sel_k = gt_k | (eq_k & (c_eq <= need))
        rank = c_gt + jnp.minimum(c_eq, need)
        rank_ref[k] = jnp.where(sel_k, rank, 0.0)

    hi = aff.astype(BF16)
    rest = aff - hi.astype(F32)
    mid = rest.astype(BF16)
    lo = (rest - mid.astype(F32)).astype(BF16)
    tok = lax.broadcasted_iota(jnp.int32, (BF16_ROWS, seq), 1)
    r_hi, r_lo = 3 * n_e, 3 * n_e + BF16_ROWS
    val_ref[0 * n_e:1 * n_e, :] = hi
    val_ref[1 * n_e:2 * n_e, :] = mid
    val_ref[2 * n_e:3 * n_e, :] = lo
    val_ref[r_hi:r_hi + BF16_ROWS, :] = lax.shift_right_logical(tok, tok_shift).astype(F32).astype(BF16)
    val_ref[r_lo:r_lo + BF16_ROWS, :] = (tok & ((1 << tok_shift) - 1)).astype(F32).astype(BF16)

    slot = (lax.broadcasted_iota(jnp.int32, (cap, LANES), 0) + 1).astype(F32)

    def per_expert(e, _):
        acc = jnp.zeros(res_ref.shape, F32)
        for c in range(n_tiles // chunk_tiles):
            for k in range(c * chunk_tiles, (c + 1) * chunk_tiles):
                hit = rank_ref[k, pl.ds(e, 1), :] == slot
                hot_ref[:, k * LANES:(k + 1) * LANES] = jnp.where(hit, 1.0, 0.0).astype(BF16)
            cols = slice(c * chunk_tiles * LANES, (c + 1) * chunk_tiles * LANES)
            acc = acc + lax.dot_general(val_ref[:, cols], hot_ref[:, cols], (((1,), (1,)), ((), ())),
                                        preferred_element_type=F32)
        res_ref[...] = acc
        tok_row = res_ref[r_hi:r_hi + 1, :] * float(1 << tok_shift) + res_ref[r_lo:r_lo + 1, :]
        gate_row = (res_ref[pl.ds(e, 1), :] + res_ref[pl.ds(n_e + e, 1), :]) + res_ref[pl.ds(2 * n_e + e, 1), :]
        idx_ref[0, pl.ds(e, 1), :] = tok_row.astype(jnp.int32)
        gate_ref[0, pl.ds(e, 1), :] = gate_row
        return 0

    lax.fori_loop(0, n_e, per_expert, 0)


def _route(logits_t, n_batch, seq, cap):
    n_e = logits_t.shape[0]
    return pl.pallas_call(
        functools.partial(_route_kernel, cap),
        grid=(n_batch,),
        in_specs=[pl.BlockSpec((n_e, seq), lambda b: (0, b))],
        out_specs=[
            pl.BlockSpec((1, n_e, cap), lambda b: (b, 0, 0)),
            pl.BlockSpec((1, n_e, cap), lambda b: (b, 0, 0)),
        ],
        out_shape=[
            jax.ShapeDtypeStruct((n_batch, n_e, cap), jnp.int32),
            jax.ShapeDtypeStruct((n_batch, n_e, cap), F32),
        ],
        scratch_shapes=[pltpu.VMEM((seq // LANES, n_e, LANES), F32),
                        pltpu.VMEM((3 * n_e + 2 * BF16_ROWS, seq), BF16),
                        pltpu.VMEM((cap, seq), BF16),
                        pltpu.VMEM((3 * n_e + 2 * BF16_ROWS, cap), F32)],
        compiler_params=pltpu.CompilerParams(
            dimension_semantics=("arbitrary",), vmem_limit_bytes=VMEM_LIMIT),
        name="route",
    )(logits_t)


def _moe_kernel(row_stride, n_cast, idx_ref, idxp_ref, idxn_ref, gate_ref, gatep_ref, hs_ref, wg_ref, wu_ref,
                wd_ref, *refs):
    cast_src, refs = refs[:n_cast], refs[n_cast:]
    out_ref = refs[0]
    cast_dst, (xe_ref, xt_ref, yt_ref) = refs[1:1 + n_cast], refs[1 + n_cast:]
    cap, d = xe_ref.shape
    rpt = d // LANES
    e = pl.program_id(1)
    f = pl.program_id(2)
    n_e = pl.num_programs(1)
    group = 8

    def gather_row(src_idx_ref, j):
        r = pl.multiple_of(src_idx_ref[0, 0, j] * rpt, rpt)
        xt_ref[pl.ds(j, rpt, stride=row_stride), :] = hs_ref[0, pl.ds(r, rpt), :]

    def scatter_rows(src_idx_ref, src_gate_ref, js):
        rows = [pl.multiple_of(src_idx_ref[0, 0, j] * rpt, rpt) for j in js]
        vals = [out_ref[0, pl.ds(r, rpt), :] + yt_ref[pl.ds(j, rpt, stride=row_stride), :] * src_gate_ref[0, 0, j]
                for r, j in zip(rows, js)]
        for r, v in zip(rows, vals):
            out_ref[0, pl.ds(r, rpt), :] = v

    def ffn_half():
        _cast_blocks(cast_src, cast_dst)
        x = xe_ref[...]
        g = _dot(x, wg_ref[0])
        u = _dot(x, wu_ref[0])
        hid = (jax.nn.silu(g) * u).astype(BF16)
        return _dot(hid, wd_ref[0])

    @pl.when((e == 0) & (f == 0))
    def _():
        out_ref[...] = jnp.zeros_like(out_ref)
        yt_ref[...] = jnp.zeros_like(yt_ref)

        def gather(jj, _):
            for u in range(group):
                gather_row(idx_ref, jj * group + u)
            return 0

        lax.fori_loop(0, cap // group, gather, 0)

    @pl.when(f == 0)
    def _():
        for q in range(rpt):
            xe_ref[:, q * LANES:(q + 1) * LANES] = xt_ref[pl.ds(q * row_stride, cap), :].astype(BF16)
        for j0 in range(0, cap, group):
            scatter_rows(idxp_ref, gatep_ref, range(j0, j0 + group))
        y = ffn_half()
        for q in range(rpt):
            yt_ref[pl.ds(q * row_stride, cap), :] = y[:, q * LANES:(q + 1) * LANES]

    @pl.when(f == 1)
    def _():
        for j in range(cap):
            gather_row(idxn_ref, j)
        y = ffn_half()
        for q in range(rpt):
            yt_ref[pl.ds(q * row_stride, cap), :] += y[:, q * LANES:(q + 1) * LANES]

    @pl.when((e == n_e - 1) & (f == 1))
    def _():
        def scatter(jj, _):
            scatter_rows(idx_ref, gate_ref, [jj * group + u for u in range(group)])
            return 0

        lax.fori_loop(0, cap // group, scatter, 0)


def _moe(hs3, idx, gates, w_gate, w_up, w_down, cast):
    n_batch, srows, _ = hs3.shape
    n_e, d, d_exp = w_gate.shape
    cap = idx.shape[-1]
    fc = d_exp // 2
    rpt = d // LANES
    row_stride = cap + SUBLANES
    n_be = n_batch * n_e
    idx3 = idx.reshape(n_be, 1, cap)
    gates3 = jnp.concatenate([gates.reshape(n_be, 1, cap), jnp.zeros((1, 1, cap), F32)], axis=0)

    def smem(index_map):
        return pl.BlockSpec((1, 1, cap), index_map, memory_space=pltpu.SMEM)

    c_ops, c_in, c_out, c_shapes = ([], [], [], []) if cast is None else _cast_plan(
        cast[0], cast[1], n_be * 2, lambda b, e, f: (b * n_e + e) * 2 + f)
    return pl.pallas_call(
        functools.partial(_moe_kernel, row_stride, len(c_ops)),
        grid=(n_batch, n_e, 2),
        in_specs=[
            smem(lambda b, e, f: (b * n_e + e, 0, 0)),
            smem(lambda b, e, f: (jnp.maximum(b * n_e + e - 1, 0), 0, 0)),
            smem(lambda b, e, f: (jnp.minimum(b * n_e + e + 1, n_be - 1), 0, 0)),
            smem(lambda b, e, f: (b * n_e + e, 0, 0)),
            smem(lambda b, e, f: (jnp.where(e == 0, n_be, b * n_e + e - 1), 0, 0)),
            pl.BlockSpec((1, srows, LANES), lambda b, e, f: (b, 0, 0), pipeline_mode=pl.Buffered(1)),
            pl.BlockSpec((1, d, fc), lambda b, e, f: (e, 0, f)),
            pl.BlockSpec((1, d, fc), lambda b, e, f: (e, 0, f)),
            pl.BlockSpec((1, fc, d), lambda b, e, f: (e, f, 0)),
        ] + c_in,
        out_specs=[pl.BlockSpec((1, srows, LANES), lambda b, e, f: (b, 0, 0), pipeline_mode=pl.Buffered(1))]
        + c_out,
        out_shape=[jax.ShapeDtypeStruct((n_batch, srows, LANES), F32)] + c_shapes,
        scratch_shapes=[
            pltpu.VMEM((cap, d), BF16),
            pltpu.VMEM((rpt * row_stride, LANES), F32),
            pltpu.VMEM((rpt * row_stride, LANES), F32),
        ],
        compiler_params=pltpu.CompilerParams(
            dimension_semantics=("arbitrary", "arbitrary", "arbitrary"), vmem_limit_bytes=VMEM_LIMIT),
        name="moe",
    )(idx3, idx3, idx3, gates3, gates3, hs3, w_gate, w_up, w_down, *c_ops)


def _ln2_kernel(alpha, h_ref, f_ref, g_ref, b_ref, o_ref):
    tm, d = h_ref.shape
    o_ref[...] = _ln(alpha * h_ref[...] + _slab_rows(f_ref, tm, d // LANES), g_ref[...], b_ref[...])


def _ln2(h2d, ffn_slab, alpha, g, b):
    t, d = h2d.shape
    tm = LN_TM
    rpt = d // LANES
    return pl.pallas_call(
        functools.partial(_ln2_kernel, alpha),
        grid=(t // tm,),
        in_specs=[
            pl.BlockSpec((tm, d), lambda i: (i, 0)),
            pl.BlockSpec((tm * rpt, LANES), lambda i: (i, 0)),
            pl.BlockSpec((1, d), lambda i: (0, 0)),
            pl.BlockSpec((1, d), lambda i: (0, 0)),
        ],
        out_specs=pl.BlockSpec((tm, d), lambda i: (i, 0)),
        out_shape=jax.ShapeDtypeStruct((t, d), F32),
        compiler_params=pltpu.CompilerParams(dimension_semantics=("arbitrary",)),
        name="ln2",
    )(h2d, ffn_slab, g, b)


def kernel(x, in_ln_g, in_ln_b, w_in, b_in, pool_w, pool_scale, sgu_ln_g, sgu_ln_b, sgu_w, sgu_b, p_a, p_b,
           w_out, ln1_g, ln1_b, w_router, w_gate, w_up, w_down, ln2_g, ln2_b):
    n_batch, seq, d = x.shape
    depth = w_in.shape[0]
    n_e = w_router.shape[-1]
    cap = CAPACITY_FACTOR * seq // n_e
    alpha = (2 * depth) ** 0.25
    t = n_batch * seq
    hd = sgu_ln_g.shape[-1] // SGU_HEADS
    rpt = d // LANES

    def row(a):
        return a.reshape(1, -1)

    stacked = (w_gate, w_up, w_down)
    h = x.reshape(t, d)
    ffn = None
    experts = None
    pre_g, pre_b = in_ln_g, in_ln_b
    for l in range(depth):
        sgu_bias = jnp.repeat(sgu_b[l].T, hd, axis=1)
        h, hs, logits_t, *cast_out = _mixer(
            h, ffn, (stacked, 0) if l == 0 else None, seq, alpha, row(pre_g), row(pre_b),
            w_in[l].astype(BF16), row(b_in[l]), pool_w[l].astype(BF16), row(pool_scale[l]),
            row(sgu_ln_g[l]), row(sgu_ln_b[l]), sgu_w[l].astype(BF16), sgu_bias,
            p_a[l].astype(BF16), p_b[l].astype(BF16), w_out[l].astype(BF16),
            row(ln1_g[l]), row(ln1_b[l]), w_router[l].T.astype(BF16))
        if l == 0:
            experts = [c.reshape(w.shape[1:]) for c, w in zip(cast_out, stacked)]
        idx, gates = _route(logits_t, n_batch, seq, cap)
        ffn, *cast_out = _moe(hs.reshape(n_batch, seq * rpt, LANES), idx, gates, *experts,
                              (stacked, l + 1) if l + 1 < depth else None)
        experts = [c.reshape(w.shape[1:]) for c, w in zip(cast_out, stacked)]
        ffn = ffn.reshape(t * rpt, LANES)
        pre_g, pre_b = ln2_g[l], ln2_b[l]
    out = _ln2(h, ffn, alpha, row(pre_g), row(pre_b))
    return out.reshape(n_batch, seq, d)
```

```python
import functools

import jax
import jax.numpy as jnp
from jax import lax
from jax.experimental import pallas as pl
from jax.experimental.pallas import tpu as pltpu

F32 = jnp.float32
BF16 = jnp.bfloat16

POOL_WINDOWS = (2, 4, 8, 16)
POOL_HALO = 8
SGU_CHUNK = 128
SGU_HEADS = 8
CAPACITY_FACTOR = 2
LN_EPS = 1e-5
LANES = 128
SUBLANES = 8
VMEM_LIMIT = 60 * 1024 * 1024

MIX_TM = 512
LN_TM = 512


def _ln(x, g, b):
    mu = jnp.mean(x, axis=-1, keepdims=True)
    xc = x - mu
    var = jnp.mean(xc * xc, axis=-1, keepdims=True)
    return xc * lax.rsqrt(var + LN_EPS) * g + b


def _dot(a, b):
    return jnp.dot(a, b, preferred_element_type=F32)


def _slab_rows(ref, n, rpt):
    return jnp.concatenate([ref[pl.ds(q, n, stride=rpt), :] for q in range(rpt)], axis=1)


def _cast_plan(stacked, layer, n_steps, step_of):
    ops, in_specs, out_specs, out_shapes = [], [], [], []
    for w in stacked:
        n_l, n_e, r, c = w.shape
        rows = n_e * r // n_steps
        ops.append(w.reshape(n_l * n_steps, rows, c))

        def in_map(*ids, _l=layer):
            return (_l * n_steps + step_of(*ids), 0, 0)

        def out_map(*ids):
            return (step_of(*ids), 0, 0)

        in_specs.append(pl.BlockSpec((1, rows, c), in_map))
        out_specs.append(pl.BlockSpec((1, rows, c), out_map))
        out_shapes.append(jax.ShapeDtypeStruct((n_steps, rows, c), BF16))
    return ops, in_specs, out_specs, out_shapes


def _cast_blocks(src_refs, dst_refs):
    for s, o in zip(src_refs, dst_refs):
        o[...] = s[...].astype(BF16)


def _mixer_kernel(has_ffn, n_cast, seq, alpha, *refs):
    if has_ffn:
        x_ref, xp_ref, xn_ref, f_ref, fp_ref, fn_ref = refs[:6]
        refs = refs[6:]
    else:
        x_ref, xp_ref, xn_ref = refs[:3]
        refs = refs[3:]
    cast_src, refs = refs[:n_cast], refs[n_cast:]
    (preg_ref, preb_ref, win_ref, bin_ref, poolw_ref, pscale_ref, sg_ref, sb_ref, sw_ref, sbias_ref,
     pa_ref, pb_ref, wout_ref, l1g_ref, l1b_ref, wrt_ref,
     h_ref, hs_ref, logit_ref) = refs[:19]
    cast_dst, refs = refs[19:19 + n_cast], refs[19 + n_cast:]
    proj_ref, aext_ref, ya_ref, yb_ref = refs
    _cast_blocks(cast_src, cast_dst)
    tm, d = x_ref.shape
    rpt = d // LANES
    pw = ya_ref.shape[1]
    gd = pw // len(POOL_WINDOWS)
    sw = yb_ref.shape[1]
    hd = sw // SGU_HEADS
    o_u, o_v, o_ga, o_gb = pw, pw + sw, pw + 2 * sw, pw + 2 * sw + d

    i = pl.program_id(0)
    tiles_per_seq = seq // tm
    pos = i % tiles_per_seq

    x = x_ref[...]
    xp = xp_ref[...]
    xn = xn_ref[...]
    if has_ffn:
        x = alpha * x + _slab_rows(f_ref, tm, rpt)
        xp = alpha * xp + _slab_rows(fp_ref, POOL_HALO, rpt)
        xn = alpha * xn + _slab_rows(fn_ref, POOL_HALO, rpt)
    x = _ln(x, preg_ref[...], preb_ref[...])
    xp = _ln(xp, preg_ref[...], preb_ref[...])
    xn = _ln(xn, preg_ref[...], preb_ref[...])
    xb = x.astype(BF16)

    proj_ref[...] = _dot(xb, win_ref[...]) + bin_ref[...]
    ap = _dot(xp.astype(BF16), win_ref[:, 0:pw]) + bin_ref[:, 0:pw]
    an = _dot(xn.astype(BF16), win_ref[:, 0:pw]) + bin_ref[:, 0:pw]
    ap = jnp.where(pos == 0, 0.0, ap)
    an = jnp.where(pos == tiles_per_seq - 1, 0.0, an)
    aext_ref[0:POOL_HALO, :] = ap
    aext_ref[POOL_HALO:POOL_HALO + tm, :] = proj_ref[:, 0:pw]
    aext_ref[POOL_HALO + tm:2 * POOL_HALO + tm, :] = an

    for c in range(tm // SGU_CHUNK):
        r0 = c * SGU_CHUNK
        rows = pl.ds(r0, SGU_CHUNK)
        s = pos * tm + r0 + lax.broadcasted_iota(jnp.int32, (SGU_CHUNK, 1), 0)
        for g, w in enumerate(POOL_WINDOWS):
            cols = slice(g * gd, (g + 1) * gd)
            acc = aext_ref[pl.ds(POOL_HALO + r0 - w // 2, SGU_CHUNK), cols]
            for o in range(-w // 2 + 1, w // 2):
                acc = acc + aext_ref[pl.ds(POOL_HALO + r0 + o, SGU_CHUNK), cols]
            cnt = (jnp.minimum(s + w // 2, seq) - jnp.maximum(s - w // 2, 0)).astype(F32)
            pooled = acc / cnt - proj_ref[rows, cols]
            ya = _dot(pooled.astype(BF16), poolw_ref[g]) * pscale_ref[:, cols]
            ya_ref[rows, cols] = ya.astype(BF16)
        gu = jax.nn.gelu(proj_ref[rows, o_u:o_v])
        gv = jax.nn.gelu(proj_ref[rows, o_v:o_ga])
        vb = _ln(gv, sg_ref[...], sb_ref[...]).astype(BF16)
        for hh in range(SGU_HEADS):
            hc = slice(hh * hd, (hh + 1) * hd)
            mixed = _dot(sw_ref[hh], vb[:, hc]) + sbias_ref[:, hc]
            yb_ref[rows, hc] = (gu[:, hc] * mixed).astype(BF16)

    ta = _dot(ya_ref[...], pa_ref[...])
    tb = _dot(yb_ref[...], pb_ref[...])
    merged = (jax.nn.sigmoid(proj_ref[:, o_ga:o_gb]) * ta
              + jax.nn.sigmoid(proj_ref[:, o_gb:o_gb + d]) * tb)
    mix = _dot(merged.astype(BF16), wout_ref[...])
    h1 = _ln(alpha * x + mix, l1g_ref[...], l1b_ref[...])
    h_ref[...] = h1
    hb = h1.astype(BF16)
    logit_ref[...] = lax.dot_general(wrt_ref[...], hb, (((1,), (1,)), ((), ())), preferred_element_type=F32)

    for q in range(rpt):
        hs_ref[pl.ds(q, tm, stride=rpt), :] = h1[:, q * LANES:(q + 1) * LANES]


def _mixer(x2d, ffn_slab, cast, seq, alpha, pre_g, pre_b, w_in, b_in, pool_w, pool_scale, sgu_g, sgu_b, sgu_w,
           sgu_bias, p_a, p_b, w_out, ln1_g, ln1_b, w_rt):
    t, d = x2d.shape
    tm = MIX_TM
    rpt = d // LANES
    n_e = w_rt.shape[0]
    pw = p_a.shape[0]
    sw = p_b.shape[0]
    hb = tm // POOL_HALO
    n_hb = t // POOL_HALO
    has_ffn = ffn_slab is not None

    def full(a):
        nd = a.ndim
        return pl.BlockSpec(a.shape, lambda i, _n=nd: (0,) * _n)

    def prev_halo(i):
        return (jnp.maximum(i * hb - 1, 0), 0)

    def next_halo(i):
        return (jnp.minimum((i + 1) * hb, n_hb - 1), 0)

    acts = [x2d, x2d, x2d]
    act_specs = [
        pl.BlockSpec((tm, d), lambda i: (i, 0)),
        pl.BlockSpec((POOL_HALO, d), prev_halo),
        pl.BlockSpec((POOL_HALO, d), next_halo),
    ]
    if has_ffn:
        acts += [ffn_slab, ffn_slab, ffn_slab]
        act_specs += [
            pl.BlockSpec((tm * rpt, LANES), lambda i: (i, 0)),
            pl.BlockSpec((POOL_HALO * rpt, LANES), prev_halo),
            pl.BlockSpec((POOL_HALO * rpt, LANES), next_halo),
        ]
    weights = (pre_g, pre_b, w_in, b_in, pool_w, pool_scale, sgu_g, sgu_b, sgu_w, sgu_bias, p_a, p_b, w_out,
               ln1_g, ln1_b, w_rt)
    c_ops, c_in, c_out, c_shapes = ([], [], [], []) if cast is None else _cast_plan(
        cast[0], cast[1], t // tm, lambda i: i)
    return pl.pallas_call(
        functools.partial(_mixer_kernel, has_ffn, len(c_ops), seq, alpha),
        grid=(t // tm,),
        in_specs=act_specs + c_in + [full(a) for a in weights],
        out_specs=[
            pl.BlockSpec((tm, d), lambda i: (i, 0)),
            pl.BlockSpec((tm * rpt, LANES), lambda i: (i, 0)),
            pl.BlockSpec((n_e, tm), lambda i: (0, i)),
        ] + c_out,
        out_shape=[
            jax.ShapeDtypeStruct((t, d), F32),
            jax.ShapeDtypeStruct((t * rpt, LANES), F32),
            jax.ShapeDtypeStruct((n_e, t), F32),
        ] + c_shapes,
        scratch_shapes=[
            pltpu.VMEM((tm, w_in.shape[1]), F32),
            pltpu.VMEM((tm + 2 * POOL_HALO, pw), F32),
            pltpu.VMEM((tm, pw), BF16),
            pltpu.VMEM((tm, sw), BF16),
        ],
        compiler_params=pltpu.CompilerParams(
            dimension_semantics=("arbitrary",), vmem_limit_bytes=VMEM_LIMIT),
        name="mixer",
    )(*acts, *c_ops, *weights)


def _route_kernel(cap, logit_ref, idx_ref, gate_ref):
    n_e, seq = logit_ref.shape
    n_tiles = seq // LANES

    lg = logit_ref[...]
    ex = jnp.exp(lg - jnp.max(lg, axis=0, keepdims=True))
    aff = ex / jnp.sum(ex, axis=0, keepdims=True)

    def bit_step(k, thr):
        cand = thr | jnp.left_shift(jnp.int32(1), 30 - k)
        cand_f = pltpu.bitcast(cand, F32)
        cnt = jnp.sum(jnp.where(aff >= cand_f, 1.0, 0.0), axis=1, keepdims=True)
        return jnp.where(cnt >= cap, cand, thr)

    thr = lax.fori_loop(0, 31, bit_step, jnp.zeros((n_e, 1), jnp.int32))
    thr_f = pltpu.bitcast(thr, F32)
    gt = aff > thr_f
    eq = aff == thr_f
    need = cap - jnp.sum(jnp.where(gt, 1.0, 0.0), axis=1, keepdims=True)

    tri = (lax.broadcasted_iota(jnp.int32, (LANES, LANES), 0)
           <= lax.broadcasted_iota(jnp.int32, (LANES, LANES), 1)).astype(BF16)
    carry_gt = jnp.zeros((n_e, 1), F32)
    carry_eq = jnp.zeros((n_e, 1), F32)
    ranks = []
    for k in range(n_tiles):
        cols = slice(k * LANES, (k + 1) * LANES)
        gt_k = gt[:, cols]
        eq_k = eq[:, cols]
        c_gt = _dot(jnp.where(gt_k, 1.0, 0.0).astype(BF16), tri) + carry_gt
        c_eq = _dot(jnp.where(eq_k, 1.0, 0.0).astype(BF16), tri) + carry_eq
        carry_gt = c_gt[:, LANES - 1:LANES]
        carry_eq = c_eq[:, LANES - 1:LANES]
        sel_k = gt_k | (eq_k & (c_eq <= need))
        rank = c_gt + jnp.minimum(c_eq, need)
        ranks.append(jnp.where(sel_k, rank, 0.0))
    rank_all = jnp.concatenate(ranks, axis=1).astype(jnp.int32)

    pos_bits = seq.bit_length() - 1
    tok = lax.broadcasted_iota(jnp.int32, (n_e, seq), 1)
    key = jnp.where(rank_all > 0, tok | jnp.left_shift(tok - (rank_all - 1), pos_bits), 0)
    gate = aff
    for s in range(pos_bits):
        bit = jnp.int32(1 << (pos_bits + s))
        key_in = pltpu.roll(key, seq - (1 << s), 1)
        gate_in = pltpu.roll(gate, seq - (1 << s), 1)
        arriving = (key_in & bit) != 0
        leaving = (key & bit) != 0
        key = jnp.where(arriving, key_in, jnp.where(leaving, 0, key))
        gate = jnp.where(arriving, gate_in, gate)
    idx_ref[0] = key[:, :cap] & jnp.int32(seq - 1)
    gate_ref[0] = gate[:, :cap]


def _route(logits_t, n_batch, seq, cap):
    n_e = logits_t.shape[0]
    return pl.pallas_call(
        functools.partial(_route_kernel, cap),
        grid=(n_batch,),
        in_specs=[pl.BlockSpec((n_e, seq), lambda b: (0, b))],
        out_specs=[
            pl.BlockSpec((1, n_e, cap), lambda b: (b, 0, 0)),
            pl.BlockSpec((1, n_e, cap), lambda b: (b, 0, 0)),
        ],
        out_shape=[
            jax.ShapeDtypeStruct((n_batch, n_e, cap), jnp.int32),
            jax.ShapeDtypeStruct((n_batch, n_e, cap), F32),
        ],
        compiler_params=pltpu.CompilerParams(
            dimension_semantics=("arbitrary",), vmem_limit_bytes=VMEM_LIMIT),
        name="route",
    )(logits_t)


def _moe_kernel(row_stride, n_cast, idx_ref, idxp_ref, idxn_ref, gate_ref, gatep_ref, hs_ref, wg_ref, wu_ref,
                wd_ref, *refs):
    cast_src, refs = refs[:n_cast], refs[n_cast:]
    out_ref = refs[0]
    cast_dst, (xe_ref, xt_ref, yt_ref) = refs[1:1 + n_cast], refs[1 + n_cast:]
    cap, d = xe_ref.shape
    rpt = d // LANES
    e = pl.program_id(1)
    f = pl.program_id(2)
    n_e = pl.num_programs(1)
    group = 8

    def gather_row(src_idx_ref, j):
        r = pl.multiple_of(src_idx_ref[0, 0, j] * rpt, rpt)
        xt_ref[pl.ds(j, rpt, stride=row_stride), :] = hs_ref[0, pl.ds(r, rpt), :]

    def scatter_rows(src_idx_ref, src_gate_ref, js):
        rows = [pl.multiple_of(src_idx_ref[0, 0, j] * rpt, rpt) for j in js]
        vals = [out_ref[0, pl.ds(r, rpt), :] + yt_ref[pl.ds(j, rpt, stride=row_stride), :] * src_gate_ref[0, 0, j]
                for r, j in zip(rows, js)]
        for r, v in zip(rows, vals):
            out_ref[0, pl.ds(r, rpt), :] = v

    def ffn_half():
        _cast_blocks(cast_src, cast_dst)
        x = xe_ref[...]
        g = _dot(x, wg_ref[0])
        u = _dot(x, wu_ref[0])
        hid = (jax.nn.silu(g) * u).astype(BF16)
        return _dot(hid, wd_ref[0])

    @pl.when((e == 0) & (f == 0))
    def _():
        out_ref[...] = jnp.zeros_like(out_ref)
        yt_ref[...] = jnp.zeros_like(yt_ref)

        def gather(jj, _):
            for u in range(group):
                gather_row(idx_ref, jj * group + u)
            return 0

        lax.fori_loop(0, cap // group, gather, 0)

    @pl.when(f == 0)
    def _():
        for q in range(rpt):
            xe_ref[:, q * LANES:(q + 1) * LANES] = xt_ref[pl.ds(q * row_stride, cap), :].astype(BF16)
        for j0 in range(0, cap, group):
            scatter_rows(idxp_ref, gatep_ref, range(j0, j0 + group))
        y = ffn_half()
        for q in range(rpt):
            yt_ref[pl.ds(q * row_stride, cap), :] = y[:, q * LANES:(q + 1) * LANES]

    @pl.when(f == 1)
    def _():
        for j in range(cap):
            gather_row(idxn_ref, j)
        y = ffn_half()
        for q in range(rpt):
            yt_ref[pl.ds(q * row_stride, cap), :] += y[:, q * LANES:(q + 1) * LANES]

    @pl.when((e == n_e - 1) & (f == 1))
    def _():
        def scatter(jj, _):
            scatter_rows(idx_ref, gate_ref, [jj * group + u for u in range(group)])
            return 0

        lax.fori_loop(0, cap // group, scatter, 0)


def _moe(hs3, idx, gates, w_gate, w_up, w_down, cast):
    n_batch, srows, _ = hs3.shape
    n_e, d, d_exp = w_gate.shape
    cap = idx.shape[-1]
    fc = d_exp // 2
    rpt = d // LANES
    row_stride = cap + SUBLANES
    n_be = n_batch * n_e
    idx3 = idx.reshape(n_be, 1, cap)
    gates3 = jnp.concatenate([gates.reshape(n_be, 1, cap), jnp.zeros((1, 1, cap), F32)], axis=0)

    def smem(index_map):
        return pl.BlockSpec((1, 1, cap), index_map, memory_space=pltpu.SMEM)

    c_ops, c_in, c_out, c_shapes = ([], [], [], []) if cast is None else _cast_plan(
        cast[0], cast[1], n_be * 2, lambda b, e, f: (b * n_e + e) * 2 + f)
    return pl.pallas_call(
        functools.partial(_moe_kernel, row_stride, len(c_ops)),
        grid=(n_batch, n_e, 2),
        in_specs=[
            smem(lambda b, e, f: (b * n_e + e, 0, 0)),
            smem(lambda b, e, f: (jnp.maximum(b * n_e + e - 1, 0), 0, 0)),
            smem(lambda b, e, f: (jnp.minimum(b * n_e + e + 1, n_be - 1), 0, 0)),
            smem(lambda b, e, f: (b * n_e + e, 0, 0)),
            smem(lambda b, e, f: (jnp.where(e == 0, n_be, b * n_e + e - 1), 0, 0)),
            pl.BlockSpec((1, srows, LANES), lambda b, e, f: (b, 0, 0), pipeline_mode=pl.Buffered(1)),
            pl.BlockSpec((1, d, fc), lambda b, e, f: (e, 0, f)),
            pl.BlockSpec((1, d, fc), lambda b, e, f: (e, 0, f)),
            pl.BlockSpec((1, fc, d), lambda b, e, f: (e, f, 0)),
        ] + c_in,
        out_specs=[pl.BlockSpec((1, srows, LANES), lambda b, e, f: (b, 0, 0), pipeline_mode=pl.Buffered(1))]
        + c_out,
        out_shape=[jax.ShapeDtypeStruct((n_batch, srows, LANES), F32)] + c_shapes,
        scratch_shapes=[
            pltpu.VMEM((cap, d), BF16),
            pltpu.VMEM((rpt * row_stride, LANES), F32),
            pltpu.VMEM((rpt * row_stride, LANES), F32),
        ],
        compiler_params=pltpu.CompilerParams(
            dimension_semantics=("arbitrary", "arbitrary", "arbitrary"), vmem_limit_bytes=VMEM_LIMIT),
        name="moe",
    )(idx3, idx3, idx3, gates3, gates3, hs3, w_gate, w_up, w_down, *c_ops)


def _ln2_kernel(alpha, h_ref, f_ref, g_ref, b_ref, o_ref):
    tm, d = h_ref.shape
    o_ref[...] = _ln(alpha * h_ref[...] + _slab_rows(f_ref, tm, d // LANES), g_ref[...], b_ref[...])


def _ln2(h2d, ffn_slab, alpha, g, b):
    t, d = h2d.shape
    tm = LN_TM
    rpt = d // LANES
    return pl.pallas_call(
        functools.partial(_ln2_kernel, alpha),
        grid=(t // tm,),
        in_specs=[
            pl.BlockSpec((tm, d), lambda i: (i, 0)),
            pl.BlockSpec((tm * rpt, LANES), lambda i: (i, 0)),
            pl.BlockSpec((1, d), lambda i: (0, 0)),
            pl.BlockSpec((1, d), lambda i: (0, 0)),
        ],
        out_specs=pl.BlockSpec((tm, d), lambda i: (i, 0)),
        out_shape=jax.ShapeDtypeStruct((t, d), F32),
        compiler_params=pltpu.CompilerParams(dimension_semantics=("arbitrary",)),
        name="ln2",
    )(h2d, ffn_slab, g, b)


def kernel(x, in_ln_g, in_ln_b, w_in, b_in, pool_w, pool_scale, sgu_ln_g, sgu_ln_b, sgu_w, sgu_b, p_a, p_b,
           w_out, ln1_g, ln1_b, w_router, w_gate, w_up, w_down, ln2_g, ln2_b):
    n_batch, seq, d = x.shape
    depth = w_in.shape[0]
    n_e = w_router.shape[-1]
    cap = CAPACITY_FACTOR * seq // n_e
    alpha = (2 * depth) ** 0.25
    t = n_batch * seq
    hd = sgu_ln_g.shape[-1] // SGU_HEADS
    rpt = d // LANES

    def row(a):
        return a.reshape(1, -1)

    stacked = (w_gate, w_up, w_down)
    h = x.reshape(t, d)
    ffn = None
    experts = None
    pre_g, pre_b = in_ln_g, in_ln_b
    for l in range(depth):
        sgu_bias = jnp.repeat(sgu_b[l].T, hd, axis=1)
        h, hs, logits_t, *cast_out = _mixer(
            h, ffn, (stacked, 0) if l == 0 else None, seq, alpha, row(pre_g), row(pre_b),
            w_in[l].astype(BF16), row(b_in[l]), pool_w[l].astype(BF16), row(pool_scale[l]),
            row(sgu_ln_g[l]), row(sgu_ln_b[l]), sgu_w[l].astype(BF16), sgu_bias,
            p_a[l].astype(BF16), p_b[l].astype(BF16), w_out[l].astype(BF16),
            row(ln1_g[l]), row(ln1_b[l]), w_router[l].T.astype(BF16))
        if l == 0:
            experts = [c.reshape(w.shape[1:]) for c, w in zip(cast_out, stacked)]
        idx, gates = _route(logits_t, n_batch, seq, cap)
        ffn, *cast_out = _moe(hs.reshape(n_batch, seq * rpt, LANES), idx, gates, *experts,
                              (stacked, l + 1) if l + 1 < depth else None)
        experts = [c.reshape(w.shape[1:]) for c, w in zip(cast_out, stacked)]
        ffn = ffn.reshape(t * rpt, LANES)
        pre_g, pre_b = ln2_g[l], ln2_b[l]
    out = _ln2(h, ffn, alpha, row(pre_g), row(pre_b))
    return out.reshape(n_batch, seq, d)
```

```python
import functools

import jax
import jax.numpy as jnp
from jax import lax
from jax.experimental import pallas as pl
from jax.experimental.pallas import tpu as pltpu

F32 = jnp.float32
BF16 = jnp.bfloat16

POOL_WINDOWS = (2, 4, 8, 16)
POOL_HALO = 8
SGU_CHUNK = 128
SGU_HEADS = 8
CAPACITY_FACTOR = 2
LN_EPS = 1e-5
LANES = 128
SUBLANES = 8
VMEM_LIMIT = 60 * 1024 * 1024

MIX_TM = 512
LN_TM = 1024


def _ln(x, g, b):
    mu = jnp.mean(x, axis=-1, keepdims=True)
    xc = x - mu
    var = jnp.mean(xc * xc, axis=-1, keepdims=True)
    return xc * lax.rsqrt(var + LN_EPS) * g + b


def _dot(a, b):
    return jnp.dot(a, b, preferred_element_type=F32)


def _slab_rows(ref, n, rpt):
    return jnp.concatenate([ref[pl.ds(q, n, stride=rpt), :] for q in range(rpt)], axis=1)


EXPERT_COL_PARTS = (2, 2, 1)


def _cast_plan(stacked, layer, n_steps, step_of):
    ops, in_specs, out_specs, out_shapes = [], [], [], []
    for w, parts in zip(stacked, EXPERT_COL_PARTS):
        n_l, n_e, r, c = w.shape
        rows = n_e * r // n_steps
        assert 0 < rows <= r and r % rows == 0, "a cast block must not straddle two experts"
        per_expert = r // rows
        ops.append(w.reshape(n_l * n_steps, rows, c))

        def in_map(*ids, _l=layer):
            return (_l * n_steps + step_of(*ids), 0, 0)

        def out_map(*ids, _per=per_expert):
            lin = step_of(*ids)
            return (lin // _per, 0, lin % _per, 0)

        in_specs.append(pl.BlockSpec((1, rows, c), in_map))
        out_specs.append(pl.BlockSpec((1, parts, rows, c // parts), out_map))
        out_shapes.append(jax.ShapeDtypeStruct((n_e, parts, r, c // parts), BF16))
    return ops, in_specs, out_specs, out_shapes


def _cast_blocks(src_refs, dst_refs):
    for s, o in zip(src_refs, dst_refs):
        width = o.shape[3]
        for p in range(o.shape[1]):
            o[0, p] = s[0, :, p * width:(p + 1) * width].astype(BF16)


def _mixer_kernel(has_ffn, n_cast, seq, alpha, *refs):
    if has_ffn:
        x_ref, xp_ref, xn_ref, f_ref, fp_ref, fn_ref = refs[:6]
        refs = refs[6:]
    else:
        x_ref, xp_ref, xn_ref = refs[:3]
        refs = refs[3:]
    cast_src, refs = refs[:n_cast], refs[n_cast:]
    (preg_ref, preb_ref, win_ref, bin_ref, poolw_ref, pscale_ref, sg_ref, sb_ref, sw_ref, sbias_ref,
     pa_ref, pb_ref, wout_ref, l1g_ref, l1b_ref, wrt_ref,
     h_ref, hs_ref, logit_ref) = refs[:19]
    cast_dst, refs = refs[19:19 + n_cast], refs[19 + n_cast:]
    proj_ref, aext_ref, ya_ref, yb_ref = refs
    _cast_blocks(cast_src, cast_dst)
    tm, d = x_ref.shape
    rpt = d // LANES
    pw = ya_ref.shape[1]
    gd = pw // len(POOL_WINDOWS)
    sw = yb_ref.shape[1]
    hd = sw // SGU_HEADS
    o_u, o_v, o_ga, o_gb = pw, pw + sw, pw + 2 * sw, pw + 2 * sw + d

    i = pl.program_id(0)
    tiles_per_seq = seq // tm
    pos = i % tiles_per_seq

    x = x_ref[...]
    xp = xp_ref[...]
    xn = xn_ref[...]
    if has_ffn:
        x = alpha * x + _slab_rows(f_ref, tm, rpt)
        xp = alpha * xp + _slab_rows(fp_ref, POOL_HALO, rpt)
        xn = alpha * xn + _slab_rows(fn_ref, POOL_HALO, rpt)
    x = _ln(x, preg_ref[...], preb_ref[...])
    xp = _ln(xp, preg_ref[...], preb_ref[...])
    xn = _ln(xn, preg_ref[...], preb_ref[...])
    xb = x.astype(BF16)

    proj_ref[...] = _dot(xb, win_ref[...]) + bin_ref[...]
    ap = _dot(xp.astype(BF16), win_ref[:, 0:pw]) + bin_ref[:, 0:pw]
    an = _dot(xn.astype(BF16), win_ref[:, 0:pw]) + bin_ref[:, 0:pw]
    ap = jnp.where(pos == 0, 0.0, ap)
    an = jnp.where(pos == tiles_per_seq - 1, 0.0, an)
    aext_ref[0:POOL_HALO, :] = ap
    aext_ref[POOL_HALO:POOL_HALO + tm, :] = proj_ref[:, 0:pw]
    aext_ref[POOL_HALO + tm:2 * POOL_HALO + tm, :] = an

    for c in range(tm // SGU_CHUNK):
        r0 = c * SGU_CHUNK
        rows = pl.ds(r0, SGU_CHUNK)
        s = pos * tm + r0 + lax.broadcasted_iota(jnp.int32, (SGU_CHUNK, 1), 0)
        for g, w in enumerate(POOL_WINDOWS):
            cols = slice(g * gd, (g + 1) * gd)
            acc = aext_ref[pl.ds(POOL_HALO + r0 - w // 2, SGU_CHUNK), cols]
            for o in range(-w // 2 + 1, w // 2):
                acc = acc + aext_ref[pl.ds(POOL_HALO + r0 + o, SGU_CHUNK), cols]
            cnt = (jnp.minimum(s + w // 2, seq) - jnp.maximum(s - w // 2, 0)).astype(F32)
            pooled = acc / cnt - proj_ref[rows, cols]
            ya = _dot(pooled.astype(BF16), poolw_ref[g]) * pscale_ref[:, cols]
            ya_ref[rows, cols] = ya.astype(BF16)
        gu = jax.nn.gelu(proj_ref[rows, o_u:o_v])
        gv = jax.nn.gelu(proj_ref[rows, o_v:o_ga])
        vb = _ln(gv, sg_ref[...], sb_ref[...]).astype(BF16)
        for hh in range(SGU_HEADS):
            hc = slice(hh * hd, (hh + 1) * hd)
            mixed = _dot(sw_ref[hh], vb[:, hc]) + sbias_ref[:, hc]
            yb_ref[rows, hc] = (gu[:, hc] * mixed).astype(BF16)

    ta = _dot(ya_ref[...], pa_ref[...])
    tb = _dot(yb_ref[...], pb_ref[...])
    merged = (jax.nn.sigmoid(proj_ref[:, o_ga:o_gb]) * ta
              + jax.nn.sigmoid(proj_ref[:, o_gb:o_gb + d]) * tb)
    mix = _dot(merged.astype(BF16), wout_ref[...])
    h1 = _ln(alpha * x + mix, l1g_ref[...], l1b_ref[...])
    h_ref[...] = h1
    hb = h1.astype(BF16)
    logit_ref[...] = lax.dot_general(wrt_ref[...], hb, (((1,), (1,)), ((), ())), preferred_element_type=F32)

    for q in range(rpt):
        hs_ref[pl.ds(q, tm, stride=rpt), :] = h1[:, q * LANES:(q + 1) * LANES]


def _mixer(x2d, ffn_slab, cast, seq, alpha, pre_g, pre_b, w_in, b_in, pool_w, pool_scale, sgu_g, sgu_b, sgu_w,
           sgu_bias, p_a, p_b, w_out, ln1_g, ln1_b, w_rt):
    t, d = x2d.shape
    tm = MIX_TM
    rpt = d // LANES
    n_e = w_rt.shape[0]
    pw = p_a.shape[0]
    sw = p_b.shape[0]
    hb = tm // POOL_HALO
    n_hb = t // POOL_HALO
    has_ffn = ffn_slab is not None

    def full(a):
        nd = a.ndim
        return pl.BlockSpec(a.shape, lambda i, _n=nd: (0,) * _n)

    def prev_halo(i):
        return (jnp.maximum(i * hb - 1, 0), 0)

    def next_halo(i):
        return (jnp.minimum((i + 1) * hb, n_hb - 1), 0)

    acts = [x2d, x2d, x2d]
    act_specs = [
        pl.BlockSpec((tm, d), lambda i: (i, 0)),
        pl.BlockSpec((POOL_HALO, d), prev_halo),
        pl.BlockSpec((POOL_HALO, d), next_halo),
    ]
    if has_ffn:
        acts += [ffn_slab, ffn_slab, ffn_slab]
        act_specs += [
            pl.BlockSpec((tm * rpt, LANES), lambda i: (i, 0)),
            pl.BlockSpec((POOL_HALO * rpt, LANES), prev_halo),
            pl.BlockSpec((POOL_HALO * rpt, LANES), next_halo),
        ]
    weights = (pre_g, pre_b, w_in, b_in, pool_w, pool_scale, sgu_g, sgu_b, sgu_w, sgu_bias, p_a, p_b, w_out,
               ln1_g, ln1_b, w_rt)
    c_ops, c_in, c_out, c_shapes = ([], [], [], []) if cast is None else _cast_plan(
        cast[0], cast[1], t // tm, lambda i: i)
    return pl.pallas_call(
        functools.partial(_mixer_kernel, has_ffn, len(c_ops), seq, alpha),
        grid=(t // tm,),
        in_specs=act_specs + c_in + [full(a) for a in weights],
        out_specs=[
            pl.BlockSpec((tm, d), lambda i: (i, 0)),
            pl.BlockSpec((tm * rpt, LANES), lambda i: (i, 0)),
            pl.BlockSpec((n_e, tm), lambda i: (0, i)),
        ] + c_out,
        out_shape=[
            jax.ShapeDtypeStruct((t, d), F32),
            jax.ShapeDtypeStruct((t * rpt, LANES), F32),
            jax.ShapeDtypeStruct((n_e, t), F32),
        ] + c_shapes,
        scratch_shapes=[
            pltpu.VMEM((tm, w_in.shape[1]), F32),
            pltpu.VMEM((tm + 2 * POOL_HALO, pw), F32),
            pltpu.VMEM((tm, pw), BF16),
            pltpu.VMEM((tm, sw), BF16),
        ],
        compiler_params=pltpu.CompilerParams(
            dimension_semantics=("arbitrary",), vmem_limit_bytes=VMEM_LIMIT),
        name="mixer",
    )(*acts, *c_ops, *weights)


def _route_kernel(cap, logit_ref, idx_ref, gate_ref):
    n_e, seq = logit_ref.shape
    n_tiles = seq // LANES

    lg = logit_ref[...]
    ex = jnp.exp(lg - jnp.max(lg, axis=0, keepdims=True))
    aff = ex / jnp.sum(ex, axis=0, keepdims=True)

    def bit_step(k, thr):
        cand = thr | jnp.left_shift(jnp.int32(1), 30 - k)
        cand_f = pltpu.bitcast(cand, F32)
        cnt = jnp.sum(jnp.where(aff >= cand_f, 1.0, 0.0), axis=1, keepdims=True)
        return jnp.where(cnt >= cap, cand, thr)

    thr = lax.fori_loop(0, 31, bit_step, jnp.zeros((n_e, 1), jnp.int32))
    thr_f = pltpu.bitcast(thr, F32)
    gt = aff > thr_f
    eq = aff == thr_f
    need = cap - jnp.sum(jnp.where(gt, 1.0, 0.0), axis=1, keepdims=True)

    tri = (lax.broadcasted_iota(jnp.int32, (LANES, LANES), 0)
           <= lax.broadcasted_iota(jnp.int32, (LANES, LANES), 1)).astype(BF16)
    carry_gt = jnp.zeros((n_e, 1), F32)
    carry_eq = jnp.zeros((n_e, 1), F32)
    ranks = []
    for k in range(n_tiles):
        cols = slice(k * LANES, (k + 1) * LANES)
        gt_k = gt[:, cols]
        eq_k = eq[:, cols]
        c_gt = _dot(jnp.where(gt_k, 1.0, 0.0).astype(BF16), tri) + carry_gt
        c_eq = _dot(jnp.where(eq_k, 1.0, 0.0).astype(BF16), tri) + carry_eq
        carry_gt = c_gt[:, LANES - 1:LANES]
        carry_eq = c_eq[:, LANES - 1:LANES]
        sel_k = gt_k | (eq_k & (c_eq <= need))
        rank = c_gt + jnp.minimum(c_eq, need)
        ranks.append(jnp.where(sel_k, rank, 0.0))
    rank_all = jnp.concatenate(ranks, axis=1).astype(jnp.int32)

    pos_bits = seq.bit_length() - 1
    tok = lax.broadcasted_iota(jnp.int32, (n_e, seq), 1)
    key = jnp.where(rank_all > 0, tok | jnp.left_shift(tok - (rank_all - 1), pos_bits), 0)
    gate = aff
    for s in range(pos_bits):
        bit = jnp.int32(1 << (pos_bits + s))
        key_in = pltpu.roll(key, seq - (1 << s), 1)
        gate_in = pltpu.roll(gate, seq - (1 << s), 1)
        arriving = (key_in & bit) != 0
        leaving = (key & bit) != 0
        key = jnp.where(arriving, key_in, jnp.where(leaving, 0, key))
        gate = jnp.where(arriving, gate_in, gate)
    idx_ref[0] = key[:, :cap] & jnp.int32(seq - 1)
    gate_ref[0] = gate[:, :cap]


def _route(logits_t, n_batch, seq, cap):
    n_e = logits_t.shape[0]
    return pl.pallas_call(
        functools.partial(_route_kernel, cap),
        grid=(n_batch,),
        in_specs=[pl.BlockSpec((n_e, seq), lambda b: (0, b))],
        out_specs=[
            pl.BlockSpec((1, n_e, cap), lambda b: (b, 0, 0)),
            pl.BlockSpec((1, n_e, cap), lambda b: (b, 0, 0)),
        ],
        out_shape=[
            jax.ShapeDtypeStruct((n_batch, n_e, cap), jnp.int32),
            jax.ShapeDtypeStruct((n_batch, n_e, cap), F32),
        ],
        compiler_params=pltpu.CompilerParams(
            dimension_semantics=("arbitrary",), vmem_limit_bytes=VMEM_LIMIT),
        name="route",
    )(logits_t)


def _moe_kernel(row_stride, n_cast, idx_ref, idxp_ref, idxn_ref, gate_ref, gatep_ref, hs_ref, wg_ref, wu_ref,
                wd_ref, *refs):
    cast_src, refs = refs[:n_cast], refs[n_cast:]
    out_ref = refs[0]
    cast_dst, (xe_ref, xt_ref, yt_ref) = refs[1:1 + n_cast], refs[1 + n_cast:]
    cap, d = xe_ref.shape
    rpt = d // LANES
    e = pl.program_id(1)
    f = pl.program_id(2)
    n_e = pl.num_programs(1)
    group = 8

    def gather_row(src_idx_ref, j):
        r = pl.multiple_of(src_idx_ref[0, 0, j] * rpt, rpt)
        xt_ref[pl.ds(j, rpt, stride=row_stride), :] = hs_ref[0, pl.ds(r, rpt), :]

    def scatter_rows(src_idx_ref, src_gate_ref, js):
        rows = [pl.multiple_of(src_idx_ref[0, 0, j] * rpt, rpt) for j in js]
        vals = [out_ref[0, pl.ds(r, rpt), :] + yt_ref[pl.ds(j, rpt, stride=row_stride), :] * src_gate_ref[0, 0, j]
                for r, j in zip(rows, js)]
        for r, v in zip(rows, vals):
            out_ref[0, pl.ds(r, rpt), :] = v

    def ffn_half():
        _cast_blocks(cast_src, cast_dst)
        x = xe_ref[...]
        g = _dot(x, wg_ref[0, 0])
        u = _dot(x, wu_ref[0, 0])
        hid = (jax.nn.silu(g) * u).astype(BF16)
        return _dot(hid, wd_ref[0, 0])

    @pl.when((e == 0) & (f == 0))
    def _():
        out_ref[...] = jnp.zeros_like(out_ref)
        yt_ref[...] = jnp.zeros_like(yt_ref)

        def gather(jj, _):
            for u in range(group):
                gather_row(idx_ref, jj * group + u)
            return 0

        lax.fori_loop(0, cap // group, gather, 0)

    @pl.when(f == 0)
    def _():
        for q in range(rpt):
            xe_ref[:, q * LANES:(q + 1) * LANES] = xt_ref[pl.ds(q * row_stride, cap), :].astype(BF16)
        for j0 in range(0, cap, group):
            scatter_rows(idxp_ref, gatep_ref, range(j0, j0 + group))
        y = ffn_half()
        for q in range(rpt):
            yt_ref[pl.ds(q * row_stride, cap), :] = y[:, q * LANES:(q + 1) * LANES]

    @pl.when(f == 1)
    def _():
        for j in range(cap):
            gather_row(idxn_ref, j)
        y = ffn_half()
        for q in range(rpt):
            yt_ref[pl.ds(q * row_stride, cap), :] += y[:, q * LANES:(q + 1) * LANES]

    @pl.when((e == n_e - 1) & (f == 1))
    def _():
        def scatter(jj, _):
            scatter_rows(idx_ref, gate_ref, [jj * group + u for u in range(group)])
            return 0

        lax.fori_loop(0, cap // group, scatter, 0)


def _moe(hs3, idx, gates, w_gate, w_up, w_down, cast):
    n_batch, srows, _ = hs3.shape
    n_e, n_f, d, fc = w_gate.shape
    assert n_f == 2
    cap = idx.shape[-1]
    rpt = d // LANES
    row_stride = cap + SUBLANES
    n_be = n_batch * n_e
    idx3 = idx.reshape(n_be, 1, cap)
    gates3 = jnp.concatenate([gates.reshape(n_be, 1, cap), jnp.zeros((1, 1, cap), F32)], axis=0)

    def smem(index_map):
        return pl.BlockSpec((1, 1, cap), index_map, memory_space=pltpu.SMEM)

    c_ops, c_in, c_out, c_shapes = ([], [], [], []) if cast is None else _cast_plan(
        cast[0], cast[1], n_be * 2, lambda b, e, f: (b * n_e + e) * 2 + f)
    return pl.pallas_call(
        functools.partial(_moe_kernel, row_stride, len(c_ops)),
        grid=(n_batch, n_e, 2),
        in_specs=[
            smem(lambda b, e, f: (b * n_e + e, 0, 0)),
            smem(lambda b, e, f: (jnp.maximum(b * n_e + e - 1, 0), 0, 0)),
            smem(lambda b, e, f: (jnp.minimum(b * n_e + e + 1, n_be - 1), 0, 0)),
            smem(lambda b, e, f: (b * n_e + e, 0, 0)),
            smem(lambda b, e, f: (jnp.where(e == 0, n_be, b * n_e + e - 1), 0, 0)),
            pl.BlockSpec((1, srows, LANES), lambda b, e, f: (b, 0, 0), pipeline_mode=pl.Buffered(1)),
            pl.BlockSpec((1, 1, d, fc), lambda b, e, f: (e, f, 0, 0)),
            pl.BlockSpec((1, 1, d, fc), lambda b, e, f: (e, f, 0, 0)),
            pl.BlockSpec((1, 1, fc, d), lambda b, e, f: (e, 0, f, 0)),
        ] + c_in,
        out_specs=[pl.BlockSpec((1, srows, LANES), lambda b, e, f: (b, 0, 0), pipeline_mode=pl.Buffered(1))]
        + c_out,
        out_shape=[jax.ShapeDtypeStruct((n_batch, srows, LANES), F32)] + c_shapes,
        scratch_shapes=[
            pltpu.VMEM((cap, d), BF16),
            pltpu.VMEM((rpt * row_stride, LANES), F32),
            pltpu.VMEM((rpt * row_stride, LANES), F32),
        ],
        compiler_params=pltpu.CompilerParams(
            dimension_semantics=("arbitrary", "arbitrary", "arbitrary"), vmem_limit_bytes=VMEM_LIMIT),
        name="moe",
    )(idx3, idx3, idx3, gates3, gates3, hs3, w_gate, w_up, w_down, *c_ops)


def _ln2_kernel(alpha, h_ref, f_ref, g_ref, b_ref, o_ref):
    tm, d = h_ref.shape
    o_ref[...] = _ln(alpha * h_ref[...] + _slab_rows(f_ref, tm, d // LANES), g_ref[...], b_ref[...])


def _ln2(h2d, ffn_slab, alpha, g, b):
    t, d = h2d.shape
    tm = LN_TM
    rpt = d // LANES
    return pl.pallas_call(
        functools.partial(_ln2_kernel, alpha),
        grid=(t // tm,),
        in_specs=[
            pl.BlockSpec((tm, d), lambda i: (i, 0)),
            pl.BlockSpec((tm * rpt, LANES), lambda i: (i, 0)),
            pl.BlockSpec((1, d), lambda i: (0, 0)),
            pl.BlockSpec((1, d), lambda i: (0, 0)),
        ],
        out_specs=pl.BlockSpec((tm, d), lambda i: (i, 0)),
        out_shape=jax.ShapeDtypeStruct((t, d), F32),
        compiler_params=pltpu.CompilerParams(
            dimension_semantics=("arbitrary",), vmem_limit_bytes=VMEM_LIMIT),
        name="ln2",
    )(h2d, ffn_slab, g, b)


def kernel(x, in_ln_g, in_ln_b, w_in, b_in, pool_w, pool_scale, sgu_ln_g, sgu_ln_b, sgu_w, sgu_b, p_a, p_b,
           w_out, ln1_g, ln1_b, w_router, w_gate, w_up, w_down, ln2_g, ln2_b):
    n_batch, seq, d = x.shape
    depth = w_in.shape[0]
    n_e = w_router.shape[-1]
    cap = CAPACITY_FACTOR * seq // n_e
    alpha = (2 * depth) ** 0.25
    t = n_batch * seq
    hd = sgu_ln_g.shape[-1] // SGU_HEADS
    rpt = d // LANES

    def row(a):
        return a.reshape(1, -1)

    stacked = (w_gate, w_up, w_down)
    h = x.reshape(t, d)
    ffn = None
    experts = None
    pre_g, pre_b = in_ln_g, in_ln_b
    for l in range(depth):
        sgu_bias = jnp.repeat(sgu_b[l].T, hd, axis=1)
        h, hs, logits_t, *cast_out = _mixer(
            h, ffn, (stacked, 0) if l == 0 else None, seq, alpha, row(pre_g), row(pre_b),
            w_in[l].astype(BF16), row(b_in[l]), pool_w[l].astype(BF16), row(pool_scale[l]),
            row(sgu_ln_g[l]), row(sgu_ln_b[l]), sgu_w[l].astype(BF16), sgu_bias,
            p_a[l].astype(BF16), p_b[l].astype(BF16), w_out[l].astype(BF16),
            row(ln1_g[l]), row(ln1_b[l]), w_router[l].T.astype(BF16))
        if l == 0:
            experts = cast_out
        idx, gates = _route(logits_t, n_batch, seq, cap)
        ffn, *experts = _moe(hs.reshape(n_batch, seq * rpt, LANES), idx, gates, *experts,
                             (stacked, l + 1) if l + 1 < depth else None)
        ffn = ffn.reshape(t * rpt, LANES)
        pre_g, pre_b = ln2_g[l], ln2_b[l]
    out = _ln2(h, ffn, alpha, row(pre_g), row(pre_b))
    return out.reshape(n_batch, seq, d)
```

```python
import functools

import jax
import jax.numpy as jnp
from jax import lax
from jax.experimental import pallas as pl
from jax.experimental.pallas import tpu as pltpu

F32 = jnp.float32
BF16 = jnp.bfloat16

POOL_WINDOWS = (2, 4, 8, 16)
POOL_HALO = 8
SGU_CHUNK = 128
SGU_HEADS = 8
CAPACITY_FACTOR = 2
LN_EPS = 1e-5
LANES = 128
SUBLANES = 8
VMEM_LIMIT = 62 * 1024 * 1024

MIX_TM = 512
LN_TM = 1024


def _ln(x, g, b):
    mu = jnp.mean(x, axis=-1, keepdims=True)
    xc = x - mu
    var = jnp.mean(xc * xc, axis=-1, keepdims=True)
    return xc * lax.rsqrt(var + LN_EPS) * g + b


def _dot(a, b):
    return jnp.dot(a, b, preferred_element_type=F32)


def _slab_rows(ref, n, rpt):
    return jnp.concatenate([ref[pl.ds(q, n, stride=rpt), :] for q in range(rpt)], axis=1)


EXPERT_COL_PARTS = (2, 2, 1)


def _cast_plan(stacked, layer, n_steps, step_of):
    ops, in_specs, out_specs, out_shapes = [], [], [], []
    for w, parts in zip(stacked, EXPERT_COL_PARTS):
        n_l, n_e, r, c = w.shape
        rows = n_e * r // n_steps
        assert 0 < rows <= r and r % rows == 0, "a cast block must not straddle two experts"
        per_expert = r // rows
        ops.append(w.reshape(n_l * n_steps, rows, c))

        def in_map(*ids, _l=layer):
            return (_l * n_steps + step_of(*ids), 0, 0)

        def out_map(*ids, _per=per_expert):
            lin = step_of(*ids)
            return (lin // _per, 0, lin % _per, 0)

        in_specs.append(pl.BlockSpec((1, rows, c), in_map))
        out_specs.append(pl.BlockSpec((1, parts, rows, c // parts), out_map))
        out_shapes.append(jax.ShapeDtypeStruct((n_e, parts, r, c // parts), BF16))
    return ops, in_specs, out_specs, out_shapes


def _cast_blocks(src_refs, dst_refs):
    for s, o in zip(src_refs, dst_refs):
        width = o.shape[3]
        for p in range(o.shape[1]):
            o[0, p] = s[0, :, p * width:(p + 1) * width].astype(BF16)


def _mixer_kernel(has_ffn, n_cast, seq, alpha, *refs):
    if has_ffn:
        x_ref, xp_ref, xn_ref, f_ref, fp_ref, fn_ref = refs[:6]
        refs = refs[6:]
    else:
        x_ref, xp_ref, xn_ref = refs[:3]
        refs = refs[3:]
    cast_src, refs = refs[:n_cast], refs[n_cast:]
    (preg_ref, preb_ref, win_ref, bin_ref, poolw_ref, pscale_ref, sg_ref, sb_ref, sw_ref, sbias_ref,
     pa_ref, pb_ref, wout_ref, l1g_ref, l1b_ref, wrt_ref,
     h_ref, hs_ref, logit_ref) = refs[:19]
    cast_dst, refs = refs[19:19 + n_cast], refs[19 + n_cast:]
    proj_ref, aext_ref, ya_ref, yb_ref = refs
    _cast_blocks(cast_src, cast_dst)
    tm, d = x_ref.shape
    rpt = d // LANES
    pw = ya_ref.shape[1]
    gd = pw // len(POOL_WINDOWS)
    sw = yb_ref.shape[1]
    hd = sw // SGU_HEADS
    o_u, o_v, o_ga, o_gb = pw, pw + sw, pw + 2 * sw, pw + 2 * sw + d

    i = pl.program_id(0)
    tiles_per_seq = seq // tm
    pos = i % tiles_per_seq

    x = x_ref[...]
    xp = xp_ref[...]
    xn = xn_ref[...]
    if has_ffn:
        x = alpha * x + _slab_rows(f_ref, tm, rpt)
        xp = alpha * xp + _slab_rows(fp_ref, POOL_HALO, rpt)
        xn = alpha * xn + _slab_rows(fn_ref, POOL_HALO, rpt)
    x = _ln(x, preg_ref[...], preb_ref[...])
    xp = _ln(xp, preg_ref[...], preb_ref[...])
    xn = _ln(xn, preg_ref[...], preb_ref[...])
    xb = x.astype(BF16)

    proj_ref[...] = _dot(xb, win_ref[...]) + bin_ref[...]
    ap = _dot(xp.astype(BF16), win_ref[:, 0:pw]) + bin_ref[:, 0:pw]
    an = _dot(xn.astype(BF16), win_ref[:, 0:pw]) + bin_ref[:, 0:pw]
    ap = jnp.where(pos == 0, 0.0, ap)
    an = jnp.where(pos == tiles_per_seq - 1, 0.0, an)
    aext_ref[0:POOL_HALO, :] = ap
    aext_ref[POOL_HALO:POOL_HALO + tm, :] = proj_ref[:, 0:pw]
    aext_ref[POOL_HALO + tm:2 * POOL_HALO + tm, :] = an

    for c in range(tm // SGU_CHUNK):
        r0 = c * SGU_CHUNK
        rows = pl.ds(r0, SGU_CHUNK)
        s = pos * tm + r0 + lax.broadcasted_iota(jnp.int32, (SGU_CHUNK, 1), 0)
        for g, w in enumerate(POOL_WINDOWS):
            cols = slice(g * gd, (g + 1) * gd)
            acc = aext_ref[pl.ds(POOL_HALO + r0 - w // 2, SGU_CHUNK), cols]
            for o in range(-w // 2 + 1, w // 2):
                acc = acc + aext_ref[pl.ds(POOL_HALO + r0 + o, SGU_CHUNK), cols]
            cnt = (jnp.minimum(s + w // 2, seq) - jnp.maximum(s - w // 2, 0)).astype(F32)
            pooled = acc / cnt - proj_ref[rows, cols]
            ya = _dot(pooled.astype(BF16), poolw_ref[g]) * pscale_ref[:, cols]
            ya_ref[rows, cols] = ya.astype(BF16)
        gu = jax.nn.gelu(proj_ref[rows, o_u:o_v])
        gv = jax.nn.gelu(proj_ref[rows, o_v:o_ga])
        vb = _ln(gv, sg_ref[...], sb_ref[...]).astype(BF16)
        for hh in range(SGU_HEADS):
            hc = slice(hh * hd, (hh + 1) * hd)
            mixed = _dot(sw_ref[hh], vb[:, hc]) + sbias_ref[:, hc]
            yb_ref[rows, hc] = (gu[:, hc] * mixed).astype(BF16)

    ta = _dot(ya_ref[...], pa_ref[...])
    tb = _dot(yb_ref[...], pb_ref[...])
    merged = (jax.nn.sigmoid(proj_ref[:, o_ga:o_gb]) * ta
              + jax.nn.sigmoid(proj_ref[:, o_gb:o_gb + d]) * tb)
    mix = _dot(merged.astype(BF16), wout_ref[...])
    h1 = _ln(alpha * x + mix, l1g_ref[...], l1b_ref[...])
    h_ref[...] = h1
    hb = h1.astype(BF16)
    logit_ref[...] = lax.dot_general(wrt_ref[...], hb, (((1,), (1,)), ((), ())), preferred_element_type=F32)

    for q in range(rpt):
        hs_ref[pl.ds(q, tm, stride=rpt), :] = h1[:, q * LANES:(q + 1) * LANES]


def _mixer(x2d, ffn_slab, cast, seq, alpha, pre_g, pre_b, w_in, b_in, pool_w, pool_scale, sgu_g, sgu_b, sgu_w,
           sgu_bias, p_a, p_b, w_out, ln1_g, ln1_b, w_rt):
    t, d = x2d.shape
    tm = MIX_TM
    rpt = d // LANES
    n_e = w_rt.shape[0]
    pw = p_a.shape[0]
    sw = p_b.shape[0]
    hb = tm // POOL_HALO
    n_hb = t // POOL_HALO
    has_ffn = ffn_slab is not None

    def full(a):
        nd = a.ndim
        return pl.BlockSpec(a.shape, lambda i, _n=nd: (0,) * _n)

    def prev_halo(i):
        return (jnp.maximum(i * hb - 1, 0), 0)

    def next_halo(i):
        return (jnp.minimum((i + 1) * hb, n_hb - 1), 0)

    acts = [x2d, x2d, x2d]
    act_specs = [
        pl.BlockSpec((tm, d), lambda i: (i, 0)),
        pl.BlockSpec((POOL_HALO, d), prev_halo),
        pl.BlockSpec((POOL_HALO, d), next_halo),
    ]
    if has_ffn:
        acts += [ffn_slab, ffn_slab, ffn_slab]
        act_specs += [
            pl.BlockSpec((tm * rpt, LANES), lambda i: (i, 0)),
            pl.BlockSpec((POOL_HALO * rpt, LANES), prev_halo),
            pl.BlockSpec((POOL_HALO * rpt, LANES), next_halo),
        ]
    weights = (pre_g, pre_b, w_in, b_in, pool_w, pool_scale, sgu_g, sgu_b, sgu_w, sgu_bias, p_a, p_b, w_out,
               ln1_g, ln1_b, w_rt)
    c_ops, c_in, c_out, c_shapes = ([], [], [], []) if cast is None else _cast_plan(
        cast[0], cast[1], t // tm, lambda i: i)
    return pl.pallas_call(
        functools.partial(_mixer_kernel, has_ffn, len(c_ops), seq, alpha),
        grid=(t // tm,),
        in_specs=act_specs + c_in + [full(a) for a in weights],
        out_specs=[
            pl.BlockSpec((tm, d), lambda i: (i, 0)),
            pl.BlockSpec((tm * rpt, LANES), lambda i: (i, 0)),
            pl.BlockSpec((n_e, tm), lambda i: (0, i)),
        ] + c_out,
        out_shape=[
            jax.ShapeDtypeStruct((t, d), F32),
            jax.ShapeDtypeStruct((t * rpt, LANES), F32),
            jax.ShapeDtypeStruct((n_e, t), F32),
        ] + c_shapes,
        scratch_shapes=[
            pltpu.VMEM((tm, w_in.shape[1]), F32),
            pltpu.VMEM((tm + 2 * POOL_HALO, pw), F32),
            pltpu.VMEM((tm, pw), BF16),
            pltpu.VMEM((tm, sw), BF16),
        ],
        compiler_params=pltpu.CompilerParams(
            dimension_semantics=("arbitrary",), vmem_limit_bytes=VMEM_LIMIT),
        name="mixer",
    )(*acts, *c_ops, *weights)


def _route_kernel(cap, logit_ref, idx_ref, gate_ref):
    n_e, seq = logit_ref.shape
    n_tiles = seq // LANES

    lg = logit_ref[...]
    ex = jnp.exp(lg - jnp.max(lg, axis=0, keepdims=True))
    aff = ex / jnp.sum(ex, axis=0, keepdims=True)

    def bit_step(k, thr):
        cand = thr | jnp.left_shift(jnp.int32(1), 30 - k)
        cand_f = pltpu.bitcast(cand, F32)
        cnt = jnp.sum(jnp.where(aff >= cand_f, 1.0, 0.0), axis=1, keepdims=True)
        return jnp.where(cnt >= cap, cand, thr)

    thr = lax.fori_loop(0, 31, bit_step, jnp.zeros((n_e, 1), jnp.int32))
    thr_f = pltpu.bitcast(thr, F32)
    gt = aff > thr_f
    eq = aff == thr_f
    need = cap - jnp.sum(jnp.where(gt, 1.0, 0.0), axis=1, keepdims=True)

    tri = (lax.broadcasted_iota(jnp.int32, (LANES, LANES), 0)
           <= lax.broadcasted_iota(jnp.int32, (LANES, LANES), 1)).astype(BF16)
    carry_gt = jnp.zeros((n_e, 1), F32)
    carry_eq = jnp.zeros((n_e, 1), F32)
    ranks = []
    for k in range(n_tiles):
        cols = slice(k * LANES, (k + 1) * LANES)
        gt_k = gt[:, cols]
        eq_k = eq[:, cols]
        c_gt = _dot(jnp.where(gt_k, 1.0, 0.0).astype(BF16), tri) + carry_gt
        c_eq = _dot(jnp.where(eq_k, 1.0, 0.0).astype(BF16), tri) + carry_eq
        carry_gt = c_gt[:, LANES - 1:LANES]
        carry_eq = c_eq[:, LANES - 1:LANES]
        sel_k = gt_k | (eq_k & (c_eq <= need))
        rank = c_gt + jnp.minimum(c_eq, need)
        ranks.append(jnp.where(sel_k, rank, 0.0))
    rank_all = jnp.concatenate(ranks, axis=1).astype(jnp.int32)

    pos_bits = seq.bit_length() - 1
    tok = lax.broadcasted_iota(jnp.int32, (n_e, seq), 1)
    key = jnp.where(rank_all > 0, tok | jnp.left_shift(tok - (rank_all - 1), pos_bits), 0)
    gate = aff
    for s in range(pos_bits):
        bit = jnp.int32(1 << (pos_bits + s))
        key_in = pltpu.roll(key, seq - (1 << s), 1)
        gate_in = pltpu.roll(gate, seq - (1 << s), 1)
        arriving = (key_in & bit) != 0
        leaving = (key & bit) != 0
        key = jnp.where(arriving, key_in, jnp.where(leaving, 0, key))
        gate = jnp.where(arriving, gate_in, gate)
    idx_ref[0] = key[:, :cap] & jnp.int32(seq - 1)
    gate_ref[0] = gate[:, :cap]


def _route(logits_t, n_batch, seq, cap):
    n_e = logits_t.shape[0]
    return pl.pallas_call(
        functools.partial(_route_kernel, cap),
        grid=(n_batch,),
        in_specs=[pl.BlockSpec((n_e, seq), lambda b: (0, b))],
        out_specs=[
            pl.BlockSpec((1, n_e, cap), lambda b: (b, 0, 0)),
            pl.BlockSpec((1, n_e, cap), lambda b: (b, 0, 0)),
        ],
        out_shape=[
            jax.ShapeDtypeStruct((n_batch, n_e, cap), jnp.int32),
            jax.ShapeDtypeStruct((n_batch, n_e, cap), F32),
        ],
        compiler_params=pltpu.CompilerParams(
            dimension_semantics=("arbitrary",), vmem_limit_bytes=VMEM_LIMIT),
        name="route",
    )(logits_t)


def _moe_kernel(row_stride, n_cast, idx_ref, idxp_ref, idxn_ref, gate_ref, gatep_ref, hs_ref, wg_hbm, wu_hbm,
                wd_hbm, *refs):
    cast_src, refs = refs[:n_cast], refs[n_cast:]
    out_ref = refs[0]
    cast_dst = refs[1:1 + n_cast]
    xe_ref, xt_ref, yt_ref, wg_buf, wu_buf, wd_buf, sem = refs[1 + n_cast:]
    cap, d = xe_ref.shape
    fc = wg_buf.shape[2]
    rpt = d // LANES
    b = pl.program_id(0)
    e = pl.program_id(1)
    n_b = pl.num_programs(0)
    n_e = pl.num_programs(1)
    group = 8

    def weight_copies(expert, half):
        return (pltpu.make_async_copy(wg_hbm.at[expert, half], wg_buf.at[half], sem.at[half, 0]),
                pltpu.make_async_copy(wu_hbm.at[expert, half], wu_buf.at[half], sem.at[half, 1]),
                pltpu.make_async_copy(wd_hbm.at[expert, 0, pl.ds(half * fc, fc)], wd_buf.at[half],
                                      sem.at[half, 2]))

    def gather_row(src_idx_ref, j):
        r = pl.multiple_of(src_idx_ref[0, 0, j] * rpt, rpt)
        xt_ref[pl.ds(j, rpt, stride=row_stride), :] = hs_ref[0, pl.ds(r, rpt), :]

    def scatter_rows(src_idx_ref, src_gate_ref, js):
        rows = [pl.multiple_of(src_idx_ref[0, 0, j] * rpt, rpt) for j in js]
        vals = [out_ref[0, pl.ds(r, rpt), :] + yt_ref[pl.ds(j, rpt, stride=row_stride), :] * src_gate_ref[0, 0, j]
                for r, j in zip(rows, js)]
        for r, v in zip(rows, vals):
            out_ref[0, pl.ds(r, rpt), :] = v

    def ffn_half(half):
        x = xe_ref[...]
        g = _dot(x, wg_buf[half])
        u = _dot(x, wu_buf[half])
        hid = (jax.nn.silu(g) * u).astype(BF16)
        return _dot(hid, wd_buf[half])

    first_step = (b == 0) & (e == 0)
    last_step = (b == n_b - 1) & (e == n_e - 1)

    @pl.when(first_step)
    def _():
        for c in weight_copies(e, 0):
            c.start()

    for c in weight_copies(e, 1):
        c.start()

    @pl.when(e == 0)
    def _():
        out_ref[...] = jnp.zeros_like(out_ref)
        yt_ref[...] = jnp.zeros_like(yt_ref)

        def gather(jj, _):
            for u in range(group):
                gather_row(idx_ref, jj * group + u)
            return 0

        lax.fori_loop(0, cap // group, gather, 0)

    for c in weight_copies(e, 0):
        c.wait()

    _cast_blocks(cast_src, cast_dst)
    for q in range(rpt):
        xe_ref[:, q * LANES:(q + 1) * LANES] = xt_ref[pl.ds(q * row_stride, cap), :].astype(BF16)
    for j0 in range(0, cap, group):
        scatter_rows(idxp_ref, gatep_ref, range(j0, j0 + group))
    y = ffn_half(0)
    for q in range(rpt):
        yt_ref[pl.ds(q * row_stride, cap), :] = y[:, q * LANES:(q + 1) * LANES]

    @pl.when(jnp.logical_not(last_step))
    def _():
        for c in weight_copies(jnp.where(e + 1 < n_e, e + 1, 0), 0):
            c.start()

    for c in weight_copies(e, 1):
        c.wait()

    for j in range(cap):
        gather_row(idxn_ref, j)
    y = ffn_half(1)
    for q in range(rpt):
        yt_ref[pl.ds(q * row_stride, cap), :] += y[:, q * LANES:(q + 1) * LANES]

    @pl.when(e == n_e - 1)
    def _():
        def scatter(jj, _):
            scatter_rows(idx_ref, gate_ref, [jj * group + u for u in range(group)])
            return 0

        lax.fori_loop(0, cap // group, scatter, 0)


def _moe(hs3, idx, gates, w_gate, w_up, w_down, cast):
    n_batch, srows, _ = hs3.shape
    n_e, n_f, d, fc = w_gate.shape
    assert n_f == 2
    cap = idx.shape[-1]
    rpt = d // LANES
    row_stride = cap + SUBLANES
    n_be = n_batch * n_e
    idx3 = idx.reshape(n_be, 1, cap)
    gates3 = jnp.concatenate([gates.reshape(n_be, 1, cap), jnp.zeros((1, 1, cap), F32)], axis=0)

    def smem(index_map):
        return pl.BlockSpec((1, 1, cap), index_map, memory_space=pltpu.SMEM)

    hbm = pl.BlockSpec(memory_space=pl.ANY)
    c_ops, c_in, c_out, c_shapes = ([], [], [], []) if cast is None else _cast_plan(
        cast[0], cast[1], n_be, lambda b, e: b * n_e + e)
    return pl.pallas_call(
        functools.partial(_moe_kernel, row_stride, len(c_ops)),
        grid=(n_batch, n_e),
        in_specs=[
            smem(lambda b, e: (b * n_e + e, 0, 0)),
            smem(lambda b, e: (jnp.maximum(b * n_e + e - 1, 0), 0, 0)),
            smem(lambda b, e: (jnp.minimum(b * n_e + e + 1, n_be - 1), 0, 0)),
            smem(lambda b, e: (b * n_e + e, 0, 0)),
            smem(lambda b, e: (jnp.where(e == 0, n_be, b * n_e + e - 1), 0, 0)),
            pl.BlockSpec((1, srows, LANES), lambda b, e: (b, 0, 0), pipeline_mode=pl.Buffered(1)),
            hbm, hbm, hbm,
        ] + c_in,
        out_specs=[pl.BlockSpec((1, srows, LANES), lambda b, e: (b, 0, 0), pipeline_mode=pl.Buffered(1))]
        + c_out,
        out_shape=[jax.ShapeDtypeStruct((n_batch, srows, LANES), F32)] + c_shapes,
        scratch_shapes=[
            pltpu.VMEM((cap, d), BF16),
            pltpu.VMEM((rpt * row_stride, LANES), F32),
            pltpu.VMEM((rpt * row_stride, LANES), F32),
            pltpu.VMEM((2, d, fc), BF16),
            pltpu.VMEM((2, d, fc), BF16),
            pltpu.VMEM((2, fc, d), BF16),
            pltpu.SemaphoreType.DMA((2, 3)),
        ],
        compiler_params=pltpu.CompilerParams(
            dimension_semantics=("arbitrary", "arbitrary"), vmem_limit_bytes=VMEM_LIMIT),
        name="moe",
    )(idx3, idx3, idx3, gates3, gates3, hs3, w_gate, w_up, w_down, *c_ops)


def _ln2_kernel(alpha, h_ref, f_ref, g_ref, b_ref, o_ref):
    tm, d = h_ref.shape
    o_ref[...] = _ln(alpha * h_ref[...] + _slab_rows(f_ref, tm, d // LANES), g_ref[...], b_ref[...])


def _ln2(h2d, ffn_slab, alpha, g, b):
    t, d = h2d.shape
    tm = LN_TM
    rpt = d // LANES
    return pl.pallas_call(
        functools.partial(_ln2_kernel, alpha),
        grid=(t // tm,),
        in_specs=[
            pl.BlockSpec((tm, d), lambda i: (i, 0)),
            pl.BlockSpec((tm * rpt, LANES), lambda i: (i, 0)),
            pl.BlockSpec((1, d), lambda i: (0, 0)),
            pl.BlockSpec((1, d), lambda i: (0, 0)),
        ],
        out_specs=pl.BlockSpec((tm, d), lambda i: (i, 0)),
        out_shape=jax.ShapeDtypeStruct((t, d), F32),
        compiler_params=pltpu.CompilerParams(
            dimension_semantics=("arbitrary",), vmem_limit_bytes=VMEM_LIMIT),
        name="ln2",
    )(h2d, ffn_slab, g, b)


def kernel(x, in_ln_g, in_ln_b, w_in, b_in, pool_w, pool_scale, sgu_ln_g, sgu_ln_b, sgu_w, sgu_b, p_a, p_b,
           w_out, ln1_g, ln1_b, w_router, w_gate, w_up, w_down, ln2_g, ln2_b):
    n_batch, seq, d = x.shape
    depth = w_in.shape[0]
    n_e = w_router.shape[-1]
    cap = CAPACITY_FACTOR * seq // n_e
    alpha = (2 * depth) ** 0.25
    t = n_batch * seq
    hd = sgu_ln_g.shape[-1] // SGU_HEADS
    rpt = d // LANES

    def row(a):
        return a.reshape(1, -1)

    stacked = (w_gate, w_up, w_down)
    h = x.reshape(t, d)
    ffn = None
    experts = None
    pre_g, pre_b = in_ln_g, in_ln_b
    for l in range(depth):
        sgu_bias = jnp.repeat(sgu_b[l].T, hd, axis=1)
        h, hs, logits_t, *cast_out = _mixer(
            h, ffn, (stacked, 0) if l == 0 else None, seq, alpha, row(pre_g), row(pre_b),
            w_in[l].astype(BF16), row(b_in[l]), pool_w[l].astype(BF16), row(pool_scale[l]),
            row(sgu_ln_g[l]), row(sgu_ln_b[l]), sgu_w[l].astype(BF16), sgu_bias,
            p_a[l].astype(BF16), p_b[l].astype(BF16), w_out[l].astype(BF16),
            row(ln1_g[l]), row(ln1_b[l]), w_router[l].T.astype(BF16))
        if l == 0:
            experts = cast_out
        idx, gates = _route(logits_t, n_batch, seq, cap)
        ffn, *experts = _moe(hs.reshape(n_batch, seq * rpt, LANES), idx, gates, *experts,
                             (stacked, l + 1) if l + 1 < depth else None)
        ffn = ffn.reshape(t * rpt, LANES)
        pre_g, pre_b = ln2_g[l], ln2_b[l]
    out = _ln2(h, ffn, alpha, row(pre_g), row(pre_b))
    return out.reshape(n_batch, seq, d)
```

```python
import functools

import jax
import jax.numpy as jnp
from jax import lax
from jax.experimental import pallas as pl
from jax.experimental.pallas import tpu as pltpu

F32 = jnp.float32
BF16 = jnp.bfloat16

POOL_WINDOWS = (2, 4, 8, 16)
POOL_HALO = 8
SGU_CHUNK = 128
SGU_HEADS = 8
CAPACITY_FACTOR = 2
LN_EPS = 1e-5
LANES = 128
SUBLANES = 8
VMEM_LIMIT = 62 * 1024 * 1024
MIX_TM = 512
LN_TM = 1024


def _ln(x, g, b):
    mu = jnp.mean(x, axis=-1, keepdims=True)
    xc = x - mu
    var = jnp.mean(xc * xc, axis=-1, keepdims=True)
    return xc * lax.rsqrt(var + LN_EPS) * g + b


def _dot(a, b):
    return jnp.dot(a, b, preferred_element_type=F32)


def _slab_rows(ref, n, rpt):
    return jnp.concatenate([ref[pl.ds(q, n, stride=rpt), :] for q in range(rpt)], axis=1)


EXPERT_COL_PARTS = (2, 2, 1)


def _cast_plan(stacked, layer, n_steps, step_of):
    ops, in_specs, out_specs, out_shapes = [], [], [], []
    for w, parts in zip(stacked, EXPERT_COL_PARTS):
        n_l, n_e, r, c = w.shape
        rows = n_e * r // n_steps
        assert 0 < rows <= r and r % rows == 0, "a cast block must not straddle two experts"
        per_expert = r // rows
        ops.append(w.reshape(n_l * n_steps, rows, c))

        def in_map(*ids, _l=layer):
            return (_l * n_steps + step_of(*ids), 0, 0)

        def out_map(*ids, _per=per_expert):
            lin = step_of(*ids)
            return (lin // _per, 0, lin % _per, 0)

        in_specs.append(pl.BlockSpec((1, rows, c), in_map))
        out_specs.append(pl.BlockSpec((1, parts, rows, c // parts), out_map))
        out_shapes.append(jax.ShapeDtypeStruct((n_e, parts, r, c // parts), BF16))
    return ops, in_specs, out_specs, out_shapes


def _cast_blocks(src_refs, dst_refs):
    for s, o in zip(src_refs, dst_refs):
        width = o.shape[3]
        for p in range(o.shape[1]):
            o[0, p] = s[0, :, p * width:(p + 1) * width].astype(BF16)


def _mixer_kernel(has_ffn, n_cast, seq, alpha, *refs):
    if has_ffn:
        x_ref, xp_ref, xn_ref, f_ref, fp_ref, fn_ref = refs[:6]
        refs = refs[6:]
    else:
        x_ref, xp_ref, xn_ref = refs[:3]
        refs = refs[3:]
    cast_src, refs = refs[:n_cast], refs[n_cast:]
    (preg_ref, preb_ref, win_ref, bin_ref, poolw_ref, pscale_ref, sg_ref, sb_ref, sw_ref, sbias_ref,
     pa_ref, pb_ref, wout_ref, l1g_ref, l1b_ref, wrt_ref,
     h_ref, hs_ref, logit_ref) = refs[:19]
    cast_dst, refs = refs[19:19 + n_cast], refs[19 + n_cast:]
    proj_ref, aext_ref, ya_ref, yb_ref = refs
    _cast_blocks(cast_src, cast_dst)
    tm, d = x_ref.shape
    rpt = d // LANES
    pw = ya_ref.shape[1]
    gd = pw // len(POOL_WINDOWS)
    sw = yb_ref.shape[1]
    hd = sw // SGU_HEADS
    o_u, o_v, o_ga, o_gb = pw, pw + sw, pw + 2 * sw, pw + 2 * sw + d

    i = pl.program_id(0)
    tiles_per_seq = seq // tm
    pos = i % tiles_per_seq

    x = x_ref[...]
    xp = xp_ref[...]
    xn = xn_ref[...]
    if has_ffn:
        x = alpha * x + _slab_rows(f_ref, tm, rpt)
        xp = alpha * xp + _slab_rows(fp_ref, POOL_HALO, rpt)
        xn = alpha * xn + _slab_rows(fn_ref, POOL_HALO, rpt)
    x = _ln(x, preg_ref[...], preb_ref[...])
    xp = _ln(xp, preg_ref[...], preb_ref[...])
    xn = _ln(xn, preg_ref[...], preb_ref[...])
    xb = x.astype(BF16)

    proj_ref[...] = _dot(xb, win_ref[...]) + bin_ref[...]
    ap = _dot(xp.astype(BF16), win_ref[:, 0:pw]) + bin_ref[:, 0:pw]
    an = _dot(xn.astype(BF16), win_ref[:, 0:pw]) + bin_ref[:, 0:pw]
    ap = jnp.where(pos == 0, 0.0, ap)
    an = jnp.where(pos == tiles_per_seq - 1, 0.0, an)
    aext_ref[0:POOL_HALO, :] = ap
    aext_ref[POOL_HALO:POOL_HALO + tm, :] = proj_ref[:, 0:pw]
    aext_ref[POOL_HALO + tm:2 * POOL_HALO + tm, :] = an

    for c in range(tm // SGU_CHUNK):
        r0 = c * SGU_CHUNK
        rows = pl.ds(r0, SGU_CHUNK)
        s = pos * tm + r0 + lax.broadcasted_iota(jnp.int32, (SGU_CHUNK, 1), 0)
        for g, w in enumerate(POOL_WINDOWS):
            cols = slice(g * gd, (g + 1) * gd)
            acc = aext_ref[pl.ds(POOL_HALO + r0 - w // 2, SGU_CHUNK), cols]
            for o in range(-w // 2 + 1, w // 2):
                acc = acc + aext_ref[pl.ds(POOL_HALO + r0 + o, SGU_CHUNK), cols]
            cnt = (jnp.minimum(s + w // 2, seq) - jnp.maximum(s - w // 2, 0)).astype(F32)
            pooled = acc / cnt - proj_ref[rows, cols]
            ya = _dot(pooled.astype(BF16), poolw_ref[g]) * pscale_ref[:, cols]
            ya_ref[rows, cols] = ya.astype(BF16)
        gu = jax.nn.gelu(proj_ref[rows, o_u:o_v])
        gv = jax.nn.gelu(proj_ref[rows, o_v:o_ga])
        vb = _ln(gv, sg_ref[...], sb_ref[...]).astype(BF16)
        for hh in range(SGU_HEADS):
            hc = slice(hh * hd, (hh + 1) * hd)
            mixed = _dot(sw_ref[hh], vb[:, hc]) + sbias_ref[:, hc]
            yb_ref[rows, hc] = (gu[:, hc] * mixed).astype(BF16)

    ta = _dot(ya_ref[...], pa_ref[...])
    tb = _dot(yb_ref[...], pb_ref[...])
    merged = (jax.nn.sigmoid(proj_ref[:, o_ga:o_gb]) * ta
              + jax.nn.sigmoid(proj_ref[:, o_gb:o_gb + d]) * tb)
    mix = _dot(merged.astype(BF16), wout_ref[...])
    h1 = _ln(alpha * x + mix, l1g_ref[...], l1b_ref[...])
    h_ref[...] = h1
    hb = h1.astype(BF16)
    logit_ref[...] = lax.dot_general(wrt_ref[...], hb, (((1,), (1,)), ((), ())), preferred_element_type=F32)

    for q in range(rpt):
        hs_ref[pl.ds(q, tm, stride=rpt), :] = h1[:, q * LANES:(q + 1) * LANES]


def _mixer(x2d, ffn_slab, cast, seq, alpha, pre_g, pre_b, w_in, b_in, pool_w, pool_scale, sgu_g, sgu_b, sgu_w,
           sgu_bias, p_a, p_b, w_out, ln1_g, ln1_b, w_rt):
    t, d = x2d.shape
    tm = MIX_TM
    rpt = d // LANES
    n_e = w_rt.shape[0]
    pw = p_a.shape[0]
    sw = p_b.shape[0]
    hb = tm // POOL_HALO
    n_hb = t // POOL_HALO
    has_ffn = ffn_slab is not None

    def full(a):
        nd = a.ndim
        return pl.BlockSpec(a.shape, lambda i, _n=nd: (0,) * _n)

    def prev_halo(i):
        return (jnp.maximum(i * hb - 1, 0), 0)

    def next_halo(i):
        return (jnp.minimum((i + 1) * hb, n_hb - 1), 0)

    acts = [x2d, x2d, x2d]
    act_specs = [
        pl.BlockSpec((tm, d), lambda i: (i, 0)),
        pl.BlockSpec((POOL_HALO, d), prev_halo),
        pl.BlockSpec((POOL_HALO, d), next_halo),
    ]
    if has_ffn:
        acts += [ffn_slab, ffn_slab, ffn_slab]
        act_specs += [
            pl.BlockSpec((tm * rpt, LANES), lambda i: (i, 0)),
            pl.BlockSpec((POOL_HALO * rpt, LANES), prev_halo),
            pl.BlockSpec((POOL_HALO * rpt, LANES), next_halo),
        ]
    weights = (pre_g, pre_b, w_in, b_in, pool_w, pool_scale, sgu_g, sgu_b, sgu_w, sgu_bias, p_a, p_b, w_out,
               ln1_g, ln1_b, w_rt)
    c_ops, c_in, c_out, c_shapes = ([], [], [], []) if cast is None else _cast_plan(
        cast[0], cast[1], t // tm, lambda i: i)
    return pl.pallas_call(
        functools.partial(_mixer_kernel, has_ffn, len(c_ops), seq, alpha),
        grid=(t // tm,),
        in_specs=act_specs + c_in + [full(a) for a in weights],
        out_specs=[
            pl.BlockSpec((tm, d), lambda i: (i, 0)),
            pl.BlockSpec((tm * rpt, LANES), lambda i: (i, 0)),
            pl.BlockSpec((n_e, tm), lambda i: (0, i)),
        ] + c_out,
        out_shape=[
            jax.ShapeDtypeStruct((t, d), F32),
            jax.ShapeDtypeStruct((t * rpt, LANES), F32),
            jax.ShapeDtypeStruct((n_e, t), F32),
        ] + c_shapes,
        scratch_shapes=[
            pltpu.VMEM((tm, w_in.shape[1]), F32),
            pltpu.VMEM((tm + 2 * POOL_HALO, pw), F32),
            pltpu.VMEM((tm, pw), BF16),
            pltpu.VMEM((tm, sw), BF16),
        ],
        compiler_params=pltpu.CompilerParams(
            dimension_semantics=("arbitrary",), vmem_limit_bytes=VMEM_LIMIT),
        name="mixer",
    )(*acts, *c_ops, *weights)


def _route_kernel(cap, logit_ref, idx_ref, gate_ref):
    n_e, seq = logit_ref.shape
    n_tiles = seq // LANES

    lg = logit_ref[...]
    ex = jnp.exp(lg - jnp.max(lg, axis=0, keepdims=True))
    aff = ex / jnp.sum(ex, axis=0, keepdims=True)

    def bit_step(k, thr):
        cand = thr | jnp.left_shift(jnp.int32(1), 30 - k)
        cand_f = pltpu.bitcast(cand, F32)
        cnt = jnp.sum(jnp.where(aff >= cand_f, 1.0, 0.0), axis=1, keepdims=True)
        return jnp.where(cnt >= cap, cand, thr)

    thr = lax.fori_loop(0, 31, bit_step, jnp.zeros((n_e, 1), jnp.int32))
    thr_f = pltpu.bitcast(thr, F32)
    gt = aff > thr_f
    eq = aff == thr_f
    need = cap - jnp.sum(jnp.where(gt, 1.0, 0.0), axis=1, keepdims=True)

    tri = (lax.broadcasted_iota(jnp.int32, (LANES, LANES), 0)
           <= lax.broadcasted_iota(jnp.int32, (LANES, LANES), 1)).astype(BF16)
    carry_gt = jnp.zeros((n_e, 1), F32)
    carry_eq = jnp.zeros((n_e, 1), F32)
    ranks = []
    for k in range(n_tiles):
        cols = slice(k * LANES, (k + 1) * LANES)
        gt_k = gt[:, cols]
        eq_k = eq[:, cols]
        c_gt = _dot(jnp.where(gt_k, 1.0, 0.0).astype(BF16), tri) + carry_gt
        c_eq = _dot(jnp.where(eq_k, 1.0, 0.0).astype(BF16), tri) + carry_eq
        carry_gt = c_gt[:, LANES - 1:LANES]
        carry_eq = c_eq[:, LANES - 1:LANES]
        sel_k = gt_k | (eq_k & (c_eq <= need))
        rank = c_gt + jnp.minimum(c_eq, need)
        ranks.append(jnp.where(sel_k, rank, 0.0))
    rank_all = jnp.concatenate(ranks, axis=1).astype(jnp.int32)

    pos_bits = seq.bit_length() - 1
    tok = lax.broadcasted_iota(jnp.int32, (n_e, seq), 1)
    key = jnp.where(rank_all > 0, tok | jnp.left_shift(tok - (rank_all - 1), pos_bits), 0)
    gate = aff
    for s in range(pos_bits):
        bit = jnp.int32(1 << (pos_bits + s))
        key_in = pltpu.roll(key, seq - (1 << s), 1)
        gate_in = pltpu.roll(gate, seq - (1 << s), 1)
        arriving = (key_in & bit) != 0
        leaving = (key & bit) != 0
        key = jnp.where(arriving, key_in, jnp.where(leaving, 0, key))
        gate = jnp.where(arriving, gate_in, gate)
    idx_ref[0] = key[:, :cap] & jnp.int32(seq - 1)
    gate_ref[0] = gate[:, :cap]


def _route(logits_t, n_batch, seq, cap):
    n_e = logits_t.shape[0]
    return pl.pallas_call(
        functools.partial(_route_kernel, cap),
        grid=(n_batch,),
        in_specs=[pl.BlockSpec((n_e, seq), lambda b: (0, b))],
        out_specs=[
            pl.BlockSpec((1, n_e, cap), lambda b: (b, 0, 0)),
            pl.BlockSpec((1, n_e, cap), lambda b: (b, 0, 0)),
        ],
        out_shape=[
            jax.ShapeDtypeStruct((n_batch, n_e, cap), jnp.int32),
            jax.ShapeDtypeStruct((n_batch, n_e, cap), F32),
        ],
        compiler_params=pltpu.CompilerParams(
            dimension_semantics=("arbitrary",), vmem_limit_bytes=VMEM_LIMIT),
        name="route",
    )(logits_t)


def _moe_kernel(row_stride, n_cast, early_rows, idx_ref, idxp_ref, idxn_ref, gate_ref, gatep_ref, hs_hbm, wg_hbm,
                wu_hbm, wd_hbm, *refs):
    cast_src, refs = refs[:n_cast], refs[n_cast:]
    out_ref = refs[0]
    cast_dst = refs[1:1 + n_cast]
    xe_ref, xt_ref, yt_ref, hs_ref, wg_buf, wu_buf, wd_buf, sem, hs_sem = refs[1 + n_cast:]
    cap, d = xe_ref.shape
    fc = wg_buf.shape[2]
    rpt = d // LANES
    b = pl.program_id(0)
    e = pl.program_id(1)
    n_b = pl.num_programs(0)
    n_e = pl.num_programs(1)
    group = 8

    def weight_copies(expert, half):
        return (pltpu.make_async_copy(wg_hbm.at[expert, half], wg_buf.at[half], sem.at[half, 0]),
                pltpu.make_async_copy(wu_hbm.at[expert, half], wu_buf.at[half], sem.at[half, 1]),
                pltpu.make_async_copy(wd_hbm.at[expert, 0, pl.ds(half * fc, fc)], wd_buf.at[half],
                                      sem.at[half, 2]))

    def rows_copy(sequence):
        return pltpu.make_async_copy(hs_hbm.at[sequence], hs_ref, hs_sem)

    def gather_row(src_idx_ref, j):
        r = pl.multiple_of(src_idx_ref[0, 0, j] * rpt, rpt)
        xt_ref[pl.ds(j, rpt, stride=row_stride), :] = hs_ref[pl.ds(r, rpt), :]

    def scatter_rows(src_idx_ref, src_gate_ref, js):
        rows = [pl.multiple_of(src_idx_ref[0, 0, j] * rpt, rpt) for j in js]
        vals = [out_ref[0, pl.ds(r, rpt), :] + yt_ref[pl.ds(j, rpt, stride=row_stride), :] * src_gate_ref[0, 0, j]
                for r, j in zip(rows, js)]
        for r, v in zip(rows, vals):
            out_ref[0, pl.ds(r, rpt), :] = v

    def ffn_half(half):
        x = xe_ref[...]
        g = _dot(x, wg_buf[half])
        u = _dot(x, wu_buf[half])
        hid = (jax.nn.silu(g) * u).astype(BF16)
        return _dot(hid, wd_buf[half])

    first_step = (b == 0) & (e == 0)
    last_step = (b == n_b - 1) & (e == n_e - 1)

    @pl.when(first_step)
    def _():
        for c in weight_copies(e, 0):
            c.start()
        if early_rows:
            rows_copy(b).start()

    for c in weight_copies(e, 1):
        c.start()

    if early_rows:
        @pl.when((e == n_e - 1) & (b < n_b - 1))
        def _():
            rows_copy(b + 1).start()

    @pl.when(e == 0)
    def _():
        if not early_rows:
            rows_copy(b).start()
        rows_copy(b).wait()
        out_ref[...] = jnp.zeros_like(out_ref)
        yt_ref[...] = jnp.zeros_like(yt_ref)

        def gather(jj, _):
            for u in range(group):
                gather_row(idx_ref, jj * group + u)
            return 0

        lax.fori_loop(0, cap // group, gather, 0)

    for c in weight_copies(e, 0):
        c.wait()

    _cast_blocks(cast_src, cast_dst)
    for q in range(rpt):
        xe_ref[:, q * LANES:(q + 1) * LANES] = xt_ref[pl.ds(q * row_stride, cap), :].astype(BF16)
    for j0 in range(0, cap, group):
        scatter_rows(idxp_ref, gatep_ref, range(j0, j0 + group))
    y = ffn_half(0)
    for q in range(rpt):
        yt_ref[pl.ds(q * row_stride, cap), :] = y[:, q * LANES:(q + 1) * LANES]

    @pl.when(jnp.logical_not(last_step))
    def _():
        for c in weight_copies(jnp.where(e + 1 < n_e, e + 1, 0), 0):
            c.start()

    for c in weight_copies(e, 1):
        c.wait()

    def second_half(gather_next):
        if gather_next:
            for j in range(cap):
                gather_row(idxn_ref, j)
        y = ffn_half(1)
        for q in range(rpt):
            yt_ref[pl.ds(q * row_stride, cap), :] += y[:, q * LANES:(q + 1) * LANES]

    if early_rows:
        @pl.when(e < n_e - 1)
        def _():
            second_half(True)

        @pl.when(e == n_e - 1)
        def _():
            second_half(False)
    else:
        second_half(True)

    @pl.when(e == n_e - 1)
    def _():
        def scatter(jj, _):
            scatter_rows(idx_ref, gate_ref, [jj * group + u for u in range(group)])
            return 0

        lax.fori_loop(0, cap // group, scatter, 0)


def _moe(hs3, idx, gates, w_gate, w_up, w_down, cast):
    n_batch, srows, _ = hs3.shape
    n_e, n_f, d, fc = w_gate.shape
    assert n_f == 2
    cap = idx.shape[-1]
    rpt = d // LANES
    row_stride = cap + SUBLANES
    n_be = n_batch * n_e
    idx3 = idx.reshape(n_be, 1, cap)
    gates3 = jnp.concatenate([gates.reshape(n_be, 1, cap), jnp.zeros((1, 1, cap), F32)], axis=0)

    def smem(index_map):
        return pl.BlockSpec((1, 1, cap), index_map, memory_space=pltpu.SMEM)

    hbm = pl.BlockSpec(memory_space=pl.ANY)
    c_ops, c_in, c_out, c_shapes = ([], [], [], []) if cast is None else _cast_plan(
        cast[0], cast[1], n_be, lambda b, e: b * n_e + e)
    return pl.pallas_call(
        functools.partial(_moe_kernel, row_stride, len(c_ops), cast is None),
        grid=(n_batch, n_e),
        in_specs=[
            smem(lambda b, e: (b * n_e + e, 0, 0)),
            smem(lambda b, e: (jnp.maximum(b * n_e + e - 1, 0), 0, 0)),
            smem(lambda b, e: (jnp.minimum(b * n_e + e + 1, n_be - 1), 0, 0)),
            smem(lambda b, e: (b * n_e + e, 0, 0)),
            smem(lambda b, e: (jnp.where(e == 0, n_be, b * n_e + e - 1), 0, 0)),
            hbm, hbm, hbm, hbm,
        ] + c_in,
        out_specs=[pl.BlockSpec((1, srows, LANES), lambda b, e: (b, 0, 0), pipeline_mode=pl.Buffered(1))]
        + c_out,
        out_shape=[jax.ShapeDtypeStruct((n_batch, srows, LANES), F32)] + c_shapes,
        scratch_shapes=[
            pltpu.VMEM((cap, d), BF16),
            pltpu.VMEM((rpt * row_stride, LANES), F32),
            pltpu.VMEM((rpt * row_stride, LANES), F32),
            pltpu.VMEM((srows, LANES), F32),
            pltpu.VMEM((2, d, fc), BF16),
            pltpu.VMEM((2, d, fc), BF16),
            pltpu.VMEM((2, fc, d), BF16),
            pltpu.SemaphoreType.DMA((2, 3)),
            pltpu.SemaphoreType.DMA(()),
        ],
        compiler_params=pltpu.CompilerParams(
            dimension_semantics=("arbitrary", "arbitrary"), vmem_limit_bytes=VMEM_LIMIT),
        name="moe",
    )(idx3, idx3, idx3, gates3, gates3, hs3, w_gate, w_up, w_down, *c_ops)


def _ln2_kernel(alpha, h_ref, f_ref, g_ref, b_ref, o_ref):
    tm, d = h_ref.shape
    o_ref[...] = _ln(alpha * h_ref[...] + _slab_rows(f_ref, tm, d // LANES), g_ref[...], b_ref[...])


def _ln2(h2d, ffn_slab, alpha, g, b):
    t, d = h2d.shape
    tm = LN_TM
    rpt = d // LANES
    return pl.pallas_call(
        functools.partial(_ln2_kernel, alpha),
        grid=(t // tm,),
        in_specs=[
            pl.BlockSpec((tm, d), lambda i: (i, 0)),
            pl.BlockSpec((tm * rpt, LANES), lambda i: (i, 0)),
            pl.BlockSpec((1, d), lambda i: (0, 0)),
            pl.BlockSpec((1, d), lambda i: (0, 0)),
        ],
        out_specs=pl.BlockSpec((tm, d), lambda i: (i, 0)),
        out_shape=jax.ShapeDtypeStruct((t, d), F32),
        compiler_params=pltpu.CompilerParams(
            dimension_semantics=("arbitrary",), vmem_limit_bytes=VMEM_LIMIT),
        name="ln2",
    )(h2d, ffn_slab, g, b)


def kernel(x, in_ln_g, in_ln_b, w_in, b_in, pool_w, pool_scale, sgu_ln_g, sgu_ln_b, sgu_w, sgu_b, p_a, p_b,
           w_out, ln1_g, ln1_b, w_router, w_gate, w_up, w_down, ln2_g, ln2_b):
    n_batch, seq, d = x.shape
    depth = w_in.shape[0]
    n_e = w_router.shape[-1]
    cap = CAPACITY_FACTOR * seq // n_e
    alpha = (2 * depth) ** 0.25
    t = n_batch * seq
    hd = sgu_ln_g.shape[-1] // SGU_HEADS
    rpt = d // LANES

    def row(a):
        return a.reshape(1, -1)

    stacked = (w_gate, w_up, w_down)
    h = x.reshape(t, d)
    ffn = None
    experts = None
    pre_g, pre_b = in_ln_g, in_ln_b
    for l in range(depth):
        sgu_bias = jnp.repeat(sgu_b[l].T, hd, axis=1)
        h, hs, logits_t, *cast_out = _mixer(
            h, ffn, (stacked, 0) if l == 0 else None, seq, alpha, row(pre_g), row(pre_b),
            w_in[l].astype(BF16), row(b_in[l]), pool_w[l].astype(BF16), row(pool_scale[l]),
            row(sgu_ln_g[l]), row(sgu_ln_b[l]), sgu_w[l].astype(BF16), sgu_bias,
            p_a[l].astype(BF16), p_b[l].astype(BF16), w_out[l].astype(BF16),
            row(ln1_g[l]), row(ln1_b[l]), w_router[l].T.astype(BF16))
        if l == 0:
            experts = cast_out
        idx, gates = _route(logits_t, n_batch, seq, cap)
        ffn, *experts = _moe(hs.reshape(n_batch, seq * rpt, LANES), idx, gates, *experts,
                             (stacked, l + 1) if l + 1 < depth else None)
        ffn = ffn.reshape(t * rpt, LANES)
        pre_g, pre_b = ln2_g[l], ln2_b[l]
    out = _ln2(h, ffn, alpha, row(pre_g), row(pre_b))
    return out.reshape(n_batch, seq, d)
```

```python
import functools

import jax
import jax.numpy as jnp
from jax import lax
from jax.experimental import pallas as pl
from jax.experimental.pallas import tpu as pltpu

F32 = jnp.float32
BF16 = jnp.bfloat16

POOL_WINDOWS = (2, 4, 8, 16)
POOL_HALO = 8
SGU_CHUNK = 128
SGU_HEADS = 8
CAPACITY_FACTOR = 2
LN_EPS = 1e-5
LANES = 128
SUBLANES = 8
VMEM_LIMIT = 62 * 1024 * 1024
MIX_TM = 512
LN_TM = 1024


def _ln(x, g, b):
    mu = jnp.mean(x, axis=-1, keepdims=True)
    xc = x - mu
    var = jnp.mean(xc * xc, axis=-1, keepdims=True)
    return xc * lax.rsqrt(var + LN_EPS) * g + b


def _dot(a, b):
    return jnp.dot(a, b, preferred_element_type=F32)


def _slab_rows(ref, n, rpt):
    return jnp.concatenate([ref[pl.ds(q, n, stride=rpt), :] for q in range(rpt)], axis=1)


EXPERT_COL_PARTS = (2, 2, 1)


def _cast_plan(stacked, layer, n_steps, step_of):
    ops, in_specs, out_specs, out_shapes = [], [], [], []
    for w, parts in zip(stacked, EXPERT_COL_PARTS):
        n_l, n_e, r, c = w.shape
        rows = n_e * r // n_steps
        assert 0 < rows <= r and r % rows == 0, "a cast block must not straddle two experts"
        per_expert = r // rows
        ops.append(w.reshape(n_l * n_steps, rows, c))

        def in_map(*ids, _l=layer):
            return (_l * n_steps + step_of(*ids), 0, 0)

        def out_map(*ids, _per=per_expert):
            lin = step_of(*ids)
            return (lin // _per, 0, lin % _per, 0)

        in_specs.append(pl.BlockSpec((1, rows, c), in_map))
        out_specs.append(pl.BlockSpec((1, parts, rows, c // parts), out_map))
        out_shapes.append(jax.ShapeDtypeStruct((n_e, parts, r, c // parts), BF16))
    return ops, in_specs, out_specs, out_shapes


def _cast_blocks(src_refs, dst_refs):
    for s, o in zip(src_refs, dst_refs):
        width = o.shape[3]
        for p in range(o.shape[1]):
            o[0, p] = s[0, :, p * width:(p + 1) * width].astype(BF16)


def _mixer_kernel(has_ffn, n_cast, seq, alpha, *refs):
    if has_ffn:
        x_ref, xp_ref, xn_ref, f_ref, fp_ref, fn_ref = refs[:6]
        refs = refs[6:]
    else:
        x_ref, xp_ref, xn_ref = refs[:3]
        refs = refs[3:]
    cast_src, refs = refs[:n_cast], refs[n_cast:]
    (preg_ref, preb_ref, win_ref, bin_ref, poolw_ref, pscale_ref, sg_ref, sb_ref, sw_ref, sbias_ref,
     pa_ref, pb_ref, wout_ref, l1g_ref, l1b_ref, wrt_ref,
     h_ref, hs_ref, logit_ref) = refs[:19]
    cast_dst, refs = refs[19:19 + n_cast], refs[19 + n_cast:]
    proj_ref, aext_ref, ya_ref, yb_ref = refs
    _cast_blocks(cast_src, cast_dst)
    tm, d = x_ref.shape
    rpt = d // LANES
    pw = ya_ref.shape[1]
    gd = pw // len(POOL_WINDOWS)
    sw = yb_ref.shape[1]
    hd = sw // SGU_HEADS
    o_u, o_v, o_ga, o_gb = pw, pw + sw, pw + 2 * sw, pw + 2 * sw + d

    i = pl.program_id(0)
    tiles_per_seq = seq // tm
    pos = i % tiles_per_seq

    x = x_ref[...]
    xp = xp_ref[...]
    xn = xn_ref[...]
    if has_ffn:
        x = alpha * x + _slab_rows(f_ref, tm, rpt)
        xp = alpha * xp + _slab_rows(fp_ref, POOL_HALO, rpt)
        xn = alpha * xn + _slab_rows(fn_ref, POOL_HALO, rpt)
    x = _ln(x, preg_ref[...], preb_ref[...])
    xp = _ln(xp, preg_ref[...], preb_ref[...])
    xn = _ln(xn, preg_ref[...], preb_ref[...])
    xb = x.astype(BF16)

    proj_ref[...] = _dot(xb, win_ref[...]) + bin_ref[...]
    ap = _dot(xp.astype(BF16), win_ref[:, 0:pw]) + bin_ref[:, 0:pw]
    an = _dot(xn.astype(BF16), win_ref[:, 0:pw]) + bin_ref[:, 0:pw]
    ap = jnp.where(pos == 0, 0.0, ap)
    an = jnp.where(pos == tiles_per_seq - 1, 0.0, an)
    aext_ref[0:POOL_HALO, :] = ap
    aext_ref[POOL_HALO:POOL_HALO + tm, :] = proj_ref[:, 0:pw]
    aext_ref[POOL_HALO + tm:2 * POOL_HALO + tm, :] = an

    for c in range(tm // SGU_CHUNK):
        r0 = c * SGU_CHUNK
        rows = pl.ds(r0, SGU_CHUNK)
        s = pos * tm + r0 + lax.broadcasted_iota(jnp.int32, (SGU_CHUNK, 1), 0)
        for g, w in enumerate(POOL_WINDOWS):
            cols = slice(g * gd, (g + 1) * gd)
            acc = aext_ref[pl.ds(POOL_HALO + r0 - w // 2, SGU_CHUNK), cols]
            for o in range(-w // 2 + 1, w // 2):
                acc = acc + aext_ref[pl.ds(POOL_HALO + r0 + o, SGU_CHUNK), cols]
            cnt = (jnp.minimum(s + w // 2, seq) - jnp.maximum(s - w // 2, 0)).astype(F32)
            pooled = acc / cnt - proj_ref[rows, cols]
            ya = _dot(pooled.astype(BF16), poolw_ref[g]) * pscale_ref[:, cols]
            ya_ref[rows, cols] = ya.astype(BF16)
        gu = jax.nn.gelu(proj_ref[rows, o_u:o_v])
        gv = jax.nn.gelu(proj_ref[rows, o_v:o_ga])
        vb = _ln(gv, sg_ref[...], sb_ref[...]).astype(BF16)
        for hh in range(SGU_HEADS):
            hc = slice(hh * hd, (hh + 1) * hd)
            mixed = _dot(sw_ref[hh], vb[:, hc]) + sbias_ref[:, hc]
            yb_ref[rows, hc] = (gu[:, hc] * mixed).astype(BF16)

    ta = _dot(ya_ref[...], pa_ref[...])
    tb = _dot(yb_ref[...], pb_ref[...])
    merged = (jax.nn.sigmoid(proj_ref[:, o_ga:o_gb]) * ta
              + jax.nn.sigmoid(proj_ref[:, o_gb:o_gb + d]) * tb)
    mix = _dot(merged.astype(BF16), wout_ref[...])
    h1 = _ln(alpha * x + mix, l1g_ref[...], l1b_ref[...])
    h_ref[...] = h1
    hb = h1.astype(BF16)
    logit_ref[...] = lax.dot_general(wrt_ref[...], hb, (((1,), (1,)), ((), ())), preferred_element_type=F32)

    for q in range(rpt):
        hs_ref[pl.ds(q, tm, stride=rpt), :] = h1[:, q * LANES:(q + 1) * LANES]


def _mixer(x2d, ffn_slab, cast, seq, alpha, pre_g, pre_b, w_in, b_in, pool_w, pool_scale, sgu_g, sgu_b, sgu_w,
           sgu_bias, p_a, p_b, w_out, ln1_g, ln1_b, w_rt):
    t, d = x2d.shape
    tm = MIX_TM
    rpt = d // LANES
    n_e = w_rt.shape[0]
    pw = p_a.shape[0]
    sw = p_b.shape[0]
    hb = tm // POOL_HALO
    n_hb = t // POOL_HALO
    has_ffn = ffn_slab is not None

    def full(a):
        nd = a.ndim
        return pl.BlockSpec(a.shape, lambda i, _n=nd: (0,) * _n)

    def prev_halo(i):
        return (jnp.maximum(i * hb - 1, 0), 0)

    def next_halo(i):
        return (jnp.minimum((i + 1) * hb, n_hb - 1), 0)

    acts = [x2d, x2d, x2d]
    act_specs = [
        pl.BlockSpec((tm, d), lambda i: (i, 0)),
        pl.BlockSpec((POOL_HALO, d), prev_halo),
        pl.BlockSpec((POOL_HALO, d), next_halo),
    ]
    if has_ffn:
        acts += [ffn_slab, ffn_slab, ffn_slab]
        act_specs += [
            pl.BlockSpec((tm * rpt, LANES), lambda i: (i, 0)),
            pl.BlockSpec((POOL_HALO * rpt, LANES), prev_halo),
            pl.BlockSpec((POOL_HALO * rpt, LANES), next_halo),
        ]
    weights = (pre_g, pre_b, w_in, b_in, pool_w, pool_scale, sgu_g, sgu_b, sgu_w, sgu_bias, p_a, p_b, w_out,
               ln1_g, ln1_b, w_rt)
    c_ops, c_in, c_out, c_shapes = ([], [], [], []) if cast is None else _cast_plan(
        cast[0], cast[1], t // tm, lambda i: i)
    return pl.pallas_call(
        functools.partial(_mixer_kernel, has_ffn, len(c_ops), seq, alpha),
        grid=(t // tm,),
        in_specs=act_specs + c_in + [full(a) for a in weights],
        out_specs=[
            pl.BlockSpec((tm, d), lambda i: (i, 0)),
            pl.BlockSpec((tm * rpt, LANES), lambda i: (i, 0)),
            pl.BlockSpec((n_e, tm), lambda i: (0, i)),
        ] + c_out,
        out_shape=[
            jax.ShapeDtypeStruct((t, d), F32),
            jax.ShapeDtypeStruct((t * rpt, LANES), F32),
            jax.ShapeDtypeStruct((n_e, t), F32),
        ] + c_shapes,
        scratch_shapes=[
            pltpu.VMEM((tm, w_in.shape[1]), F32),
            pltpu.VMEM((tm + 2 * POOL_HALO, pw), F32),
            pltpu.VMEM((tm, pw), BF16),
            pltpu.VMEM((tm, sw), BF16),
        ],
        compiler_params=pltpu.CompilerParams(
            dimension_semantics=("arbitrary",), vmem_limit_bytes=VMEM_LIMIT),
        name="mixer",
    )(*acts, *c_ops, *weights)


def _route_kernel(cap, logit_ref, idx_ref, gate_ref):
    n_e, seq = logit_ref.shape
    n_tiles = seq // LANES

    lg = logit_ref[...]
    ex = jnp.exp(lg - jnp.max(lg, axis=0, keepdims=True))
    aff = ex / jnp.sum(ex, axis=0, keepdims=True)

    def bit_step(k, thr):
        cand = thr | jnp.left_shift(jnp.int32(1), 30 - k)
        cand_f = pltpu.bitcast(cand, F32)
        cnt = jnp.sum(jnp.where(aff >= cand_f, 1.0, 0.0), axis=1, keepdims=True)
        return jnp.where(cnt >= cap, cand, thr)

    thr = lax.fori_loop(0, 31, bit_step, jnp.zeros((n_e, 1), jnp.int32))
    thr_f = pltpu.bitcast(thr, F32)
    gt = aff > thr_f
    eq = aff == thr_f
    need = cap - jnp.sum(jnp.where(gt, 1.0, 0.0), axis=1, keepdims=True)

    tri = (lax.broadcasted_iota(jnp.int32, (LANES, LANES), 0)
           <= lax.broadcasted_iota(jnp.int32, (LANES, LANES), 1)).astype(BF16)
    carry_gt = jnp.zeros((n_e, 1), F32)
    carry_eq = jnp.zeros((n_e, 1), F32)
    ranks = []
    for k in range(n_tiles):
        cols = slice(k * LANES, (k + 1) * LANES)
        gt_k = gt[:, cols]
        eq_k = eq[:, cols]
        c_gt = _dot(jnp.where(gt_k, 1.0, 0.0).astype(BF16), tri) + carry_gt
        c_eq = _dot(jnp.where(eq_k, 1.0, 0.0).astype(BF16), tri) + carry_eq
        carry_gt = c_gt[:, LANES - 1:LANES]
        carry_eq = c_eq[:, LANES - 1:LANES]
        sel_k = gt_k | (eq_k & (c_eq <= need))
        rank = c_gt + jnp.minimum(c_eq, need)
        ranks.append(jnp.where(sel_k, rank, 0.0))
    rank_all = jnp.concatenate(ranks, axis=1).astype(jnp.int32)

    pos_bits = seq.bit_length() - 1
    tok = lax.broadcasted_iota(jnp.int32, (n_e, seq), 1)
    key = jnp.where(rank_all > 0, tok | jnp.left_shift(tok - (rank_all - 1), pos_bits), 0)
    gate = aff
    for s in range(pos_bits):
        bit = jnp.int32(1 << (pos_bits + s))
        key_in = pltpu.roll(key, seq - (1 << s), 1)
        gate_in = pltpu.roll(gate, seq - (1 << s), 1)
        arriving = (key_in & bit) != 0
        leaving = (key & bit) != 0
        key = jnp.where(arriving, key_in, jnp.where(leaving, 0, key))
        gate = jnp.where(arriving, gate_in, gate)
    idx_ref[0] = key[:, :cap] & jnp.int32(seq - 1)
    gate_ref[0] = gate[:, :cap]


def _route(logits_t, n_batch, seq, cap):
    n_e = logits_t.shape[0]
    return pl.pallas_call(
        functools.partial(_route_kernel, cap),
        grid=(n_batch,),
        in_specs=[pl.BlockSpec((n_e, seq), lambda b: (0, b))],
        out_specs=[
            pl.BlockSpec((1, n_e, cap), lambda b: (b, 0, 0)),
            pl.BlockSpec((1, n_e, cap), lambda b: (b, 0, 0)),
        ],
        out_shape=[
            jax.ShapeDtypeStruct((n_batch, n_e, cap), jnp.int32),
            jax.ShapeDtypeStruct((n_batch, n_e, cap), F32),
        ],
        compiler_params=pltpu.CompilerParams(
            dimension_semantics=("arbitrary",), vmem_limit_bytes=VMEM_LIMIT),
        name="route",
    )(logits_t)


def _moe_kernel(row_stride, n_cast, early_rows, idx_ref, idxp_ref, idxn_ref, gate_ref, gatep_ref, hs_hbm, wg_hbm,
                wu_hbm, wd_hbm, *refs):
    cast_src, refs = refs[:n_cast], refs[n_cast:]
    out_ref = refs[0]
    cast_dst = refs[1:1 + n_cast]
    xe_ref, xt_ref, yt_ref, hs_ref, wg_buf, wu_buf, wd_buf, sem, hs_sem = refs[1 + n_cast:]
    cap, d = xe_ref.shape
    fc = wg_buf.shape[2]
    rpt = d // LANES
    b = pl.program_id(0)
    e = pl.program_id(1)
    n_b = pl.num_programs(0)
    n_e = pl.num_programs(1)
    group = 8

    def weight_copies(expert, half):
        return (pltpu.make_async_copy(wg_hbm.at[expert, half], wg_buf.at[half], sem.at[half, 0]),
                pltpu.make_async_copy(wu_hbm.at[expert, half], wu_buf.at[half], sem.at[half, 1]),
                pltpu.make_async_copy(wd_hbm.at[expert, 0, pl.ds(half * fc, fc)], wd_buf.at[half],
                                      sem.at[half, 2]))

    def rows_copy(sequence):
        return pltpu.make_async_copy(hs_hbm.at[sequence], hs_ref, hs_sem)

    def gather_row(src_idx_ref, j):
        r = pl.multiple_of(src_idx_ref[0, 0, j] * rpt, rpt)
        xt_ref[pl.ds(j, rpt, stride=row_stride), :] = hs_ref[pl.ds(r, rpt), :]

    def scatter_rows(src_idx_ref, src_gate_ref, js):
        rows = [pl.multiple_of(src_idx_ref[0, 0, j] * rpt, rpt) for j in js]
        vals = [out_ref[0, pl.ds(r, rpt), :] + yt_ref[pl.ds(j, rpt, stride=row_stride), :] * src_gate_ref[0, 0, j]
                for r, j in zip(rows, js)]
        for r, v in zip(rows, vals):
            out_ref[0, pl.ds(r, rpt), :] = v

    def ffn_half(half):
        x = xe_ref[...]
        g = _dot(x, wg_buf[half])
        u = _dot(x, wu_buf[half])
        hid = (jax.nn.silu(g) * u).astype(BF16)
        return _dot(hid, wd_buf[half])

    first_step = (b == 0) & (e == 0)
    last_step = (b == n_b - 1) & (e == n_e - 1)

    @pl.when(first_step)
    def _():
        for c in weight_copies(e, 0):
            c.start()
        rows_copy(b).start()

    for c in weight_copies(e, 1):
        c.start()

    def start_next_rows():
        @pl.when((e == n_e - 1) & (b < n_b - 1))
        def _():
            rows_copy(b + 1).start()

    if early_rows:
        start_next_rows()

    @pl.when(e == 0)
    def _():
        out_ref[...] = jnp.zeros_like(out_ref)
        yt_ref[...] = jnp.zeros_like(yt_ref)
        rows_copy(b).wait()

        def gather(jj, _):
            for u in range(group):
                gather_row(idx_ref, jj * group + u)
            return 0

        lax.fori_loop(0, cap // group, gather, 0)

    for c in weight_copies(e, 0):
        c.wait()

    _cast_blocks(cast_src, cast_dst)
    for q in range(rpt):
        xe_ref[:, q * LANES:(q + 1) * LANES] = xt_ref[pl.ds(q * row_stride, cap), :].astype(BF16)
    for j0 in range(0, cap, group):
        scatter_rows(idxp_ref, gatep_ref, range(j0, j0 + group))
    y = ffn_half(0)
    for q in range(rpt):
        yt_ref[pl.ds(q * row_stride, cap), :] = y[:, q * LANES:(q + 1) * LANES]

    @pl.when(jnp.logical_not(last_step))
    def _():
        for c in weight_copies(jnp.where(e + 1 < n_e, e + 1, 0), 0):
            c.start()

    for c in weight_copies(e, 1):
        c.wait()

    def second_half(gather_next):
        if gather_next:
            for j in range(cap):
                gather_row(idxn_ref, j)
        y = ffn_half(1)
        for q in range(rpt):
            yt_ref[pl.ds(q * row_stride, cap), :] += y[:, q * LANES:(q + 1) * LANES]

    if early_rows:
        @pl.when(e < n_e - 1)
        def _():
            second_half(True)

        @pl.when(e == n_e - 1)
        def _():
            second_half(False)
    else:
        second_half(True)
        start_next_rows()

    @pl.when(e == n_e - 1)
    def _():
        def scatter(jj, _):
            scatter_rows(idx_ref, gate_ref, [jj * group + u for u in range(group)])
            return 0

        lax.fori_loop(0, cap // group, scatter, 0)


def _moe(hs3, idx, gates, w_gate, w_up, w_down, cast):
    n_batch, srows, _ = hs3.shape
    n_e, n_f, d, fc = w_gate.shape
    assert n_f == 2
    cap = idx.shape[-1]
    rpt = d // LANES
    row_stride = cap + SUBLANES
    n_be = n_batch * n_e
    idx3 = idx.reshape(n_be, 1, cap)
    gates3 = jnp.concatenate([gates.reshape(n_be, 1, cap), jnp.zeros((1, 1, cap), F32)], axis=0)

    def smem(index_map):
        return pl.BlockSpec((1, 1, cap), index_map, memory_space=pltpu.SMEM)

    hbm = pl.BlockSpec(memory_space=pl.ANY)
    c_ops, c_in, c_out, c_shapes = ([], [], [], []) if cast is None else _cast_plan(
        cast[0], cast[1], n_be, lambda b, e: b * n_e + e)
    return pl.pallas_call(
        functools.partial(_moe_kernel, row_stride, len(c_ops), cast is None),
        grid=(n_batch, n_e),
        in_specs=[
            smem(lambda b, e: (b * n_e + e, 0, 0)),
            smem(lambda b, e: (jnp.maximum(b * n_e + e - 1, 0), 0, 0)),
            smem(lambda b, e: (jnp.minimum(b * n_e + e + 1, n_be - 1), 0, 0)),
            smem(lambda b, e: (b * n_e + e, 0, 0)),
            smem(lambda b, e: (jnp.where(e == 0, n_be, b * n_e + e - 1), 0, 0)),
            hbm, hbm, hbm, hbm,
        ] + c_in,
        out_specs=[pl.BlockSpec((1, srows, LANES), lambda b, e: (b, 0, 0), pipeline_mode=pl.Buffered(1))]
        + c_out,
        out_shape=[jax.ShapeDtypeStruct((n_batch, srows, LANES), F32)] + c_shapes,
        scratch_shapes=[
            pltpu.VMEM((cap, d), BF16),
            pltpu.VMEM((rpt * row_stride, LANES), F32),
            pltpu.VMEM((rpt * row_stride, LANES), F32),
            pltpu.VMEM((srows, LANES), F32),
            pltpu.VMEM((2, d, fc), BF16),
            pltpu.VMEM((2, d, fc), BF16),
            pltpu.VMEM((2, fc, d), BF16),
            pltpu.SemaphoreType.DMA((2, 3)),
            pltpu.SemaphoreType.DMA(()),
        ],
        compiler_params=pltpu.CompilerParams(
            dimension_semantics=("arbitrary", "arbitrary"), vmem_limit_bytes=VMEM_LIMIT),
        name="moe",
    )(idx3, idx3, idx3, gates3, gates3, hs3, w_gate, w_up, w_down, *c_ops)


def _ln2_kernel(alpha, h_ref, f_ref, g_ref, b_ref, o_ref):
    tm, d = h_ref.shape
    o_ref[...] = _ln(alpha * h_ref[...] + _slab_rows(f_ref, tm, d // LANES), g_ref[...], b_ref[...])


def _ln2(h2d, ffn_slab, alpha, g, b):
    t, d = h2d.shape
    tm = LN_TM
    rpt = d // LANES
    return pl.pallas_call(
        functools.partial(_ln2_kernel, alpha),
        grid=(t // tm,),
        in_specs=[
            pl.BlockSpec((tm, d), lambda i: (i, 0)),
            pl.BlockSpec((tm * rpt, LANES), lambda i: (i, 0)),
            pl.BlockSpec((1, d), lambda i: (0, 0)),
            pl.BlockSpec((1, d), lambda i: (0, 0)),
        ],
        out_specs=pl.BlockSpec((tm, d), lambda i: (i, 0)),
        out_shape=jax.ShapeDtypeStruct((t, d), F32),
        compiler_params=pltpu.CompilerParams(
            dimension_semantics=("arbitrary",), vmem_limit_bytes=VMEM_LIMIT),
        name="ln2",
    )(h2d, ffn_slab, g, b)


def kernel(x, in_ln_g, in_ln_b, w_in, b_in, pool_w, pool_scale, sgu_ln_g, sgu_ln_b, sgu_w, sgu_b, p_a, p_b,
           w_out, ln1_g, ln1_b, w_router, w_gate, w_up, w_down, ln2_g, ln2_b):
    n_batch, seq, d = x.shape
    depth = w_in.shape[0]
    n_e = w_router.shape[-1]
    cap = CAPACITY_FACTOR * seq // n_e
    alpha = (2 * depth) ** 0.25
    t = n_batch * seq
    hd = sgu_ln_g.shape[-1] // SGU_HEADS
    rpt = d // LANES

    def row(a):
        return a.reshape(1, -1)

    stacked = (w_gate, w_up, w_down)
    h = x.reshape(t, d)
    ffn = None
    experts = None
    pre_g, pre_b = in_ln_g, in_ln_b
    for l in range(depth):
        sgu_bias = jnp.repeat(sgu_b[l].T, hd, axis=1)
        h, hs, logits_t, *cast_out = _mixer(
            h, ffn, (stacked, 0) if l == 0 else None, seq, alpha, row(pre_g), row(pre_b),
            w_in[l].astype(BF16), row(b_in[l]), pool_w[l].astype(BF16), row(pool_scale[l]),
            row(sgu_ln_g[l]), row(sgu_ln_b[l]), sgu_w[l].astype(BF16), sgu_bias,
            p_a[l].astype(BF16), p_b[l].astype(BF16), w_out[l].astype(BF16),
            row(ln1_g[l]), row(ln1_b[l]), w_router[l].T.astype(BF16))
        if l == 0:
            experts = cast_out
        idx, gates = _route(logits_t, n_batch, seq, cap)
        ffn, *experts = _moe(hs.reshape(n_batch, seq * rpt, LANES), idx, gates, *experts,
                             (stacked, l + 1) if l + 1 < depth else None)
        ffn = ffn.reshape(t * rpt, LANES)
        pre_g, pre_b = ln2_g[l], ln2_b[l]
    out = _ln2(h, ffn, alpha, row(pre_g), row(pre_b))
    return out.reshape(n_batch, seq, d)
```

```python
import functools

import jax
import jax.numpy as jnp
from jax import lax
from jax.experimental import pallas as pl
from jax.experimental.pallas import tpu as pltpu

F32 = jnp.float32
BF16 = jnp.bfloat16

POOL_WINDOWS = (2, 4, 8, 16)
POOL_HALO = 8
SGU_CHUNK = 128
SGU_HEADS = 8
CAPACITY_FACTOR = 2
LN_EPS = 1e-5
LANES = 128
SUBLANES = 8
VMEM_LIMIT = 249 * 256 * 1024
MIX_TM = 512
LN_TM = 1024


def _ln(x, g, b):
    mu = jnp.mean(x, axis=-1, keepdims=True)
    xc = x - mu
    var = jnp.mean(xc * xc, axis=-1, keepdims=True)
    return xc * lax.rsqrt(var + LN_EPS) * g + b


def _dot(a, b):
    return jnp.dot(a, b, preferred_element_type=F32)


def _slab_rows(ref, n, rpt):
    return jnp.concatenate([ref[pl.ds(q, n, stride=rpt), :] for q in range(rpt)], axis=1)


def _cast_plan(stacked, layer, n_steps, step_of):
    ops, in_specs, out_specs, out_shapes = [], [], [], []
    for w, parts in stacked:
        n_l, n_e, r, c = w.shape
        rows = n_e * r // n_steps
        assert 0 < rows <= r and r % rows == 0, "a cast block must not straddle two experts"
        per_expert = r // rows
        ops.append(w.reshape(n_l * n_steps, rows, c))

        def in_map(*ids, _l=layer):
            return (_l * n_steps + step_of(*ids), 0, 0)

        def out_map(*ids, _per=per_expert):
            lin = step_of(*ids)
            return (lin // _per, 0, lin % _per, 0)

        in_specs.append(pl.BlockSpec((1, rows, c), in_map))
        out_specs.append(pl.BlockSpec((1, parts, rows, c // parts), out_map))
        out_shapes.append(jax.ShapeDtypeStruct((n_e, parts, r, c // parts), BF16))
    return ops, in_specs, out_specs, out_shapes


def _cast_blocks(src_refs, dst_refs):
    for s, o in zip(src_refs, dst_refs):
        width = o.shape[3]
        for p in range(o.shape[1]):
            o[0, p] = s[0, :, p * width:(p + 1) * width].astype(BF16)


def _mixer_kernel(has_ffn, n_cast, seq, alpha, *refs):
    if has_ffn:
        x_ref, xp_ref, xn_ref, f_ref, fp_ref, fn_ref = refs[:6]
        refs = refs[6:]
    else:
        x_ref, xp_ref, xn_ref = refs[:3]
        refs = refs[3:]
    cast_src, refs = refs[:n_cast], refs[n_cast:]
    (preg_ref, preb_ref, win_ref, bin_ref, poolw_ref, pscale_ref, sg_ref, sb_ref, sw_ref, sbias_ref,
     pa_ref, pb_ref, wout_ref, l1g_ref, l1b_ref, wrt_ref,
     h_ref, hs_ref, logit_ref) = refs[:19]
    cast_dst, refs = refs[19:19 + n_cast], refs[19 + n_cast:]
    proj_ref, aext_ref, ya_ref, yb_ref = refs
    _cast_blocks(cast_src, cast_dst)
    tm, d = x_ref.shape
    rpt = d // LANES
    pw = ya_ref.shape[1]
    gd = pw // len(POOL_WINDOWS)
    sw = yb_ref.shape[1]
    hd = sw // SGU_HEADS
    o_u, o_v, o_ga, o_gb = pw, pw + sw, pw + 2 * sw, pw + 2 * sw + d

    i = pl.program_id(0)
    tiles_per_seq = seq // tm
    pos = i % tiles_per_seq

    x = x_ref[...]
    xp = xp_ref[...]
    xn = xn_ref[...]
    if has_ffn:
        x = alpha * x + _slab_rows(f_ref, tm, rpt)
        xp = alpha * xp + _slab_rows(fp_ref, POOL_HALO, rpt)
        xn = alpha * xn + _slab_rows(fn_ref, POOL_HALO, rpt)
    x = _ln(x, preg_ref[...], preb_ref[...])
    xp = _ln(xp, preg_ref[...], preb_ref[...])
    xn = _ln(xn, preg_ref[...], preb_ref[...])
    xb = x.astype(BF16)

    proj_ref[...] = _dot(xb, win_ref[...]) + bin_ref[...]
    ap = _dot(xp.astype(BF16), win_ref[:, 0:pw]) + bin_ref[:, 0:pw]
    an = _dot(xn.astype(BF16), win_ref[:, 0:pw]) + bin_ref[:, 0:pw]
    ap = jnp.where(pos == 0, 0.0, ap)
    an = jnp.where(pos == tiles_per_seq - 1, 0.0, an)
    aext_ref[0:POOL_HALO, :] = ap
    aext_ref[POOL_HALO:POOL_HALO + tm, :] = proj_ref[:, 0:pw]
    aext_ref[POOL_HALO + tm:2 * POOL_HALO + tm, :] = an

    for c in range(tm // SGU_CHUNK):
        r0 = c * SGU_CHUNK
        rows = pl.ds(r0, SGU_CHUNK)
        s = pos * tm + r0 + lax.broadcasted_iota(jnp.int32, (SGU_CHUNK, 1), 0)
        for g, w in enumerate(POOL_WINDOWS):
            cols = slice(g * gd, (g + 1) * gd)
            acc = aext_ref[pl.ds(POOL_HALO + r0 - w // 2, SGU_CHUNK), cols]
            for o in range(-w // 2 + 1, w // 2):
                acc = acc + aext_ref[pl.ds(POOL_HALO + r0 + o, SGU_CHUNK), cols]
            cnt = (jnp.minimum(s + w // 2, seq) - jnp.maximum(s - w // 2, 0)).astype(F32)
            pooled = acc / cnt - proj_ref[rows, cols]
            ya = _dot(pooled.astype(BF16), poolw_ref[g]) * pscale_ref[:, cols]
            ya_ref[rows, cols] = ya.astype(BF16)
        gu = jax.nn.gelu(proj_ref[rows, o_u:o_v])
        gv = jax.nn.gelu(proj_ref[rows, o_v:o_ga])
        vb = _ln(gv, sg_ref[...], sb_ref[...]).astype(BF16)
        for hh in range(SGU_HEADS):
            hc = slice(hh * hd, (hh + 1) * hd)
            mixed = _dot(sw_ref[hh], vb[:, hc]) + sbias_ref[:, hc]
            yb_ref[rows, hc] = (gu[:, hc] * mixed).astype(BF16)

    ta = _dot(ya_ref[...], pa_ref[...])
    tb = _dot(yb_ref[...], pb_ref[...])
    merged = (jax.nn.sigmoid(proj_ref[:, o_ga:o_gb]) * ta
              + jax.nn.sigmoid(proj_ref[:, o_gb:o_gb + d]) * tb)
    mix = _dot(merged.astype(BF16), wout_ref[...])
    h1 = _ln(alpha * x + mix, l1g_ref[...], l1b_ref[...])
    h_ref[...] = h1
    hb = h1.astype(BF16)
    logit_ref[...] = lax.dot_general(wrt_ref[...], hb, (((1,), (1,)), ((), ())), preferred_element_type=F32)

    for q in range(rpt):
        hs_ref[pl.ds(q, tm, stride=rpt), :] = h1[:, q * LANES:(q + 1) * LANES]


def _mixer(x2d, ffn_slab, cast, seq, alpha, pre_g, pre_b, w_in, b_in, pool_w, pool_scale, sgu_g, sgu_b, sgu_w,
           sgu_bias, p_a, p_b, w_out, ln1_g, ln1_b, w_rt):
    t, d = x2d.shape
    tm = MIX_TM
    rpt = d // LANES
    n_e = w_rt.shape[0]
    pw = p_a.shape[0]
    sw = p_b.shape[0]
    hb = tm // POOL_HALO
    n_hb = t // POOL_HALO
    has_ffn = ffn_slab is not None

    def full(a):
        nd = a.ndim
        return pl.BlockSpec(a.shape, lambda i, _n=nd: (0,) * _n)

    def prev_halo(i):
        return (jnp.maximum(i * hb - 1, 0), 0)

    def next_halo(i):
        return (jnp.minimum((i + 1) * hb, n_hb - 1), 0)

    acts = [x2d, x2d, x2d]
    act_specs = [
        pl.BlockSpec((tm, d), lambda i: (i, 0)),
        pl.BlockSpec((POOL_HALO, d), prev_halo),
        pl.BlockSpec((POOL_HALO, d), next_halo),
    ]
    if has_ffn:
        acts += [ffn_slab, ffn_slab, ffn_slab]
        act_specs += [
            pl.BlockSpec((tm * rpt, LANES), lambda i: (i, 0)),
            pl.BlockSpec((POOL_HALO * rpt, LANES), prev_halo),
            pl.BlockSpec((POOL_HALO * rpt, LANES), next_halo),
        ]
    weights = (pre_g, pre_b, w_in, b_in, pool_w, pool_scale, sgu_g, sgu_b, sgu_w, sgu_bias, p_a, p_b, w_out,
               ln1_g, ln1_b, w_rt)
    c_ops, c_in, c_out, c_shapes = ([], [], [], []) if cast is None else _cast_plan(
        cast[0], cast[1], t // tm, lambda i: i)
    return pl.pallas_call(
        functools.partial(_mixer_kernel, has_ffn, len(c_ops), seq, alpha),
        grid=(t // tm,),
        in_specs=act_specs + c_in + [full(a) for a in weights],
        out_specs=[
            pl.BlockSpec((tm, d), lambda i: (i, 0)),
            pl.BlockSpec((tm * rpt, LANES), lambda i: (i, 0)),
            pl.BlockSpec((n_e, tm), lambda i: (0, i)),
        ] + c_out,
        out_shape=[
            jax.ShapeDtypeStruct((t, d), F32),
            jax.ShapeDtypeStruct((t * rpt, LANES), F32),
            jax.ShapeDtypeStruct((n_e, t), F32),
        ] + c_shapes,
        scratch_shapes=[
            pltpu.VMEM((tm, w_in.shape[1]), F32),
            pltpu.VMEM((tm + 2 * POOL_HALO, pw), F32),
            pltpu.VMEM((tm, pw), BF16),
            pltpu.VMEM((tm, sw), BF16),
        ],
        compiler_params=pltpu.CompilerParams(
            dimension_semantics=("arbitrary",), vmem_limit_bytes=VMEM_LIMIT),
        name="mixer",
    )(*acts, *c_ops, *weights)


def _route_kernel(cap, logit_ref, idx_ref, gate_ref):
    n_e, seq = logit_ref.shape
    n_tiles = seq // LANES

    lg = logit_ref[...]
    ex = jnp.exp(lg - jnp.max(lg, axis=0, keepdims=True))
    aff = ex / jnp.sum(ex, axis=0, keepdims=True)

    def bit_step(k, thr):
        cand = thr | jnp.left_shift(jnp.int32(1), 30 - k)
        cand_f = pltpu.bitcast(cand, F32)
        cnt = jnp.sum(jnp.where(aff >= cand_f, 1.0, 0.0), axis=1, keepdims=True)
        return jnp.where(cnt >= cap, cand, thr)

    thr = lax.fori_loop(0, 31, bit_step, jnp.zeros((n_e, 1), jnp.int32))
    thr_f = pltpu.bitcast(thr, F32)
    gt = aff > thr_f
    eq = aff == thr_f
    need = cap - jnp.sum(jnp.where(gt, 1.0, 0.0), axis=1, keepdims=True)

    tri = (lax.broadcasted_iota(jnp.int32, (LANES, LANES), 0)
           <= lax.broadcasted_iota(jnp.int32, (LANES, LANES), 1)).astype(BF16)
    carry_gt = jnp.zeros((n_e, 1), F32)
    carry_eq = jnp.zeros((n_e, 1), F32)
    ranks = []
    for k in range(n_tiles):
        cols = slice(k * LANES, (k + 1) * LANES)
        gt_k = gt[:, cols]
        eq_k = eq[:, cols]
        c_gt = _dot(jnp.where(gt_k, 1.0, 0.0).astype(BF16), tri) + carry_gt
        c_eq = _dot(jnp.where(eq_k, 1.0, 0.0).astype(BF16), tri) + carry_eq
        carry_gt = c_gt[:, LANES - 1:LANES]
        carry_eq = c_eq[:, LANES - 1:LANES]
        sel_k = gt_k | (eq_k & (c_eq <= need))
        rank = c_gt + jnp.minimum(c_eq, need)
        ranks.append(jnp.where(sel_k, rank, 0.0))
    rank_all = jnp.concatenate(ranks, axis=1).astype(jnp.int32)

    pos_bits = seq.bit_length() - 1
    tok = lax.broadcasted_iota(jnp.int32, (n_e, seq), 1)
    key = jnp.where(rank_all > 0, tok | jnp.left_shift(tok - (rank_all - 1), pos_bits), 0)
    gate = aff
    for s in range(pos_bits):
        bit = jnp.int32(1 << (pos_bits + s))
        key_in = pltpu.roll(key, seq - (1 << s), 1)
        gate_in = pltpu.roll(gate, seq - (1 << s), 1)
        arriving = (key_in & bit) != 0
        leaving = (key & bit) != 0
        key = jnp.where(arriving, key_in, jnp.where(leaving, 0, key))
        gate = jnp.where(arriving, gate_in, gate)
    idx_ref[0] = key[:, :cap] & jnp.int32(seq - 1)
    gate_ref[0] = gate[:, :cap]


def _route(logits_t, n_batch, seq, cap):
    n_e = logits_t.shape[0]
    return pl.pallas_call(
        functools.partial(_route_kernel, cap),
        grid=(n_batch,),
        in_specs=[pl.BlockSpec((n_e, seq), lambda b: (0, b))],
        out_specs=[
            pl.BlockSpec((1, n_e, cap), lambda b: (b, 0, 0)),
            pl.BlockSpec((1, n_e, cap), lambda b: (b, 0, 0)),
        ],
        out_shape=[
            jax.ShapeDtypeStruct((n_batch, n_e, cap), jnp.int32),
            jax.ShapeDtypeStruct((n_batch, n_e, cap), F32),
        ],
        compiler_params=pltpu.CompilerParams(
            dimension_semantics=("arbitrary",), vmem_limit_bytes=VMEM_LIMIT),
        name="route",
    )(logits_t)


def _moe_kernel(row_stride, n_cast, idx_ref, idxp_ref, idxn_ref, gate_ref, gatep_ref, hs_hbm, wg_hbm, wu_hbm,
                wd_hbm, *refs):
    cast_src, refs = refs[:n_cast], refs[n_cast:]
    out_ref = refs[0]
    cast_dst = refs[1:1 + n_cast]
    xe_ref, xt_ref, yt_ref, hs_ref, wg_buf, wu_buf, wd_buf, sem, hs_sem = refs[1 + n_cast:]
    cap, d = xe_ref.shape
    fc = wg_buf.shape[2]
    rpt = d // LANES
    b = pl.program_id(0)
    e = pl.program_id(1)
    n_b = pl.num_programs(0)
    n_e = pl.num_programs(1)
    group = 8

    def weight_copies(expert, half):
        return (pltpu.make_async_copy(wg_hbm.at[expert, half], wg_buf.at[half], sem.at[half, 0]),
                pltpu.make_async_copy(wu_hbm.at[expert, half], wu_buf.at[half], sem.at[half, 1]),
                pltpu.make_async_copy(wd_hbm.at[expert, 0, pl.ds(half * fc, fc)], wd_buf.at[half],
                                      sem.at[half, 2]))

    def rows_copy(sequence):
        return pltpu.make_async_copy(hs_hbm.at[sequence], hs_ref, hs_sem)

    def gather_row(src_idx_ref, j):
        r = pl.multiple_of(src_idx_ref[0, 0, j] * rpt, rpt)
        xt_ref[pl.ds(j, rpt, stride=row_stride), :] = hs_ref[pl.ds(r, rpt), :]

    def scatter_rows(src_idx_ref, src_gate_ref, js):
        rows = [pl.multiple_of(src_idx_ref[0, 0, j] * rpt, rpt) for j in js]
        vals = [out_ref[0, pl.ds(r, rpt), :] + yt_ref[pl.ds(j, rpt, stride=row_stride), :] * src_gate_ref[0, 0, j]
                for r, j in zip(rows, js)]
        for r, v in zip(rows, vals):
            out_ref[0, pl.ds(r, rpt), :] = v

    def ffn_half(half):
        x = xe_ref[...]
        g = _dot(x, wg_buf[half])
        u = _dot(x, wu_buf[half])
        hid = (jax.nn.silu(g) * u).astype(BF16)
        return _dot(hid, wd_buf[half])

    first_step = (b == 0) & (e == 0)
    last_step = (b == n_b - 1) & (e == n_e - 1)

    @pl.when(first_step)
    def _():
        for c in weight_copies(e, 0):
            c.start()
        rows_copy(b).start()

    for c in weight_copies(e, 1):
        c.start()

    @pl.when((e == n_e - 1) & (b < n_b - 1))
    def _():
        rows_copy(b + 1).start()

    @pl.when(e == 0)
    def _():
        out_ref[...] = jnp.zeros_like(out_ref)
        yt_ref[...] = jnp.zeros_like(yt_ref)
        rows_copy(b).wait()

        def gather(jj, _):
            for u in range(group):
                gather_row(idx_ref, jj * group + u)
            return 0

        lax.fori_loop(0, cap // group, gather, 0)

    for c in weight_copies(e, 0):
        c.wait()

    _cast_blocks(cast_src, cast_dst)
    for q in range(rpt):
        xe_ref[:, q * LANES:(q + 1) * LANES] = xt_ref[pl.ds(q * row_stride, cap), :].astype(BF16)
    for j0 in range(0, cap, group):
        scatter_rows(idxp_ref, gatep_ref, range(j0, j0 + group))
    y = ffn_half(0)
    for q in range(rpt):
        yt_ref[pl.ds(q * row_stride, cap), :] = y[:, q * LANES:(q + 1) * LANES]

    @pl.when(jnp.logical_not(last_step))
    def _():
        for c in weight_copies(jnp.where(e + 1 < n_e, e + 1, 0), 0):
            c.start()

    for c in weight_copies(e, 1):
        c.wait()

    def second_half(gather_next):
        if gather_next:
            for j in range(cap):
                gather_row(idxn_ref, j)
        y = ffn_half(1)
        for q in range(rpt):
            yt_ref[pl.ds(q * row_stride, cap), :] += y[:, q * LANES:(q + 1) * LANES]

    @pl.when(e < n_e - 1)
    def _():
        second_half(True)

    @pl.when(e == n_e - 1)
    def _():
        second_half(False)

    @pl.when(e == n_e - 1)
    def _():
        def scatter(jj, _):
            scatter_rows(idx_ref, gate_ref, [jj * group + u for u in range(group)])
            return 0

        lax.fori_loop(0, cap // group, scatter, 0)


def _moe(hs3, idx, gates, w_gate, w_up, w_down, cast):
    n_batch, srows, _ = hs3.shape
    n_e, n_f, d, fc = w_gate.shape
    assert n_f == 2
    cap = idx.shape[-1]
    rpt = d // LANES
    row_stride = cap + SUBLANES
    n_be = n_batch * n_e
    idx3 = idx.reshape(n_be, 1, cap)
    gates3 = jnp.concatenate([gates.reshape(n_be, 1, cap), jnp.zeros((1, 1, cap), F32)], axis=0)

    def smem(index_map):
        return pl.BlockSpec((1, 1, cap), index_map, memory_space=pltpu.SMEM)

    hbm = pl.BlockSpec(memory_space=pl.ANY)
    c_ops, c_in, c_out, c_shapes = ([], [], [], []) if cast is None else _cast_plan(
        cast[0], cast[1], n_be, lambda b, e: b * n_e + e)
    return pl.pallas_call(
        functools.partial(_moe_kernel, row_stride, len(c_ops)),
        grid=(n_batch, n_e),
        in_specs=[
            smem(lambda b, e: (b * n_e + e, 0, 0)),
            smem(lambda b, e: (jnp.maximum(b * n_e + e - 1, 0), 0, 0)),
            smem(lambda b, e: (jnp.minimum(b * n_e + e + 1, n_be - 1), 0, 0)),
            smem(lambda b, e: (b * n_e + e, 0, 0)),
            smem(lambda b, e: (jnp.where(e == 0, n_be, b * n_e + e - 1), 0, 0)),
            hbm, hbm, hbm, hbm,
        ] + c_in,
        out_specs=[pl.BlockSpec((1, srows, LANES), lambda b, e: (b, 0, 0), pipeline_mode=pl.Buffered(1))]
        + c_out,
        out_shape=[jax.ShapeDtypeStruct((n_batch, srows, LANES), F32)] + c_shapes,
        scratch_shapes=[
            pltpu.VMEM((cap, d), BF16),
            pltpu.VMEM((rpt * row_stride, LANES), F32),
            pltpu.VMEM((rpt * row_stride, LANES), F32),
            pltpu.VMEM((srows, LANES), F32),
            pltpu.VMEM((2, d, fc), BF16),
            pltpu.VMEM((2, d, fc), BF16),
            pltpu.VMEM((2, fc, d), BF16),
            pltpu.SemaphoreType.DMA((2, 3)),
            pltpu.SemaphoreType.DMA(()),
        ],
        compiler_params=pltpu.CompilerParams(
            dimension_semantics=("arbitrary", "arbitrary"), vmem_limit_bytes=VMEM_LIMIT),
        name="moe",
    )(idx3, idx3, idx3, gates3, gates3, hs3, w_gate, w_up, w_down, *c_ops)


def _ln2_kernel(alpha, h_ref, f_ref, g_ref, b_ref, o_ref):
    tm, d = h_ref.shape
    o_ref[...] = _ln(alpha * h_ref[...] + _slab_rows(f_ref, tm, d // LANES), g_ref[...], b_ref[...])


def _ln2(h2d, ffn_slab, alpha, g, b):
    t, d = h2d.shape
    tm = LN_TM
    rpt = d // LANES
    return pl.pallas_call(
        functools.partial(_ln2_kernel, alpha),
        grid=(t // tm,),
        in_specs=[
            pl.BlockSpec((tm, d), lambda i: (i, 0)),
            pl.BlockSpec((tm * rpt, LANES), lambda i: (i, 0)),
            pl.BlockSpec((1, d), lambda i: (0, 0)),
            pl.BlockSpec((1, d), lambda i: (0, 0)),
        ],
        out_specs=pl.BlockSpec((tm, d), lambda i: (i, 0)),
        out_shape=jax.ShapeDtypeStruct((t, d), F32),
        compiler_params=pltpu.CompilerParams(
            dimension_semantics=("arbitrary",), vmem_limit_bytes=VMEM_LIMIT),
        name="ln2",
    )(h2d, ffn_slab, g, b)


def kernel(x, in_ln_g, in_ln_b, w_in, b_in, pool_w, pool_scale, sgu_ln_g, sgu_ln_b, sgu_w, sgu_b, p_a, p_b,
           w_out, ln1_g, ln1_b, w_router, w_gate, w_up, w_down, ln2_g, ln2_b):
    n_batch, seq, d = x.shape
    depth = w_in.shape[0]
    n_e = w_router.shape[-1]
    cap = CAPACITY_FACTOR * seq // n_e
    alpha = (2 * depth) ** 0.25
    t = n_batch * seq
    hd = sgu_ln_g.shape[-1] // SGU_HEADS
    rpt = d // LANES

    def row(a):
        return a.reshape(1, -1)

    gate_up = ((w_gate, 2), (w_up, 2))
    down = ((w_down, 1),)
    h = x.reshape(t, d)
    ffn = None
    experts = []
    pre_g, pre_b = in_ln_g, in_ln_b
    for l in range(depth):
        sgu_bias = jnp.repeat(sgu_b[l].T, hd, axis=1)
        h, hs, logits_t, *cast_out = _mixer(
            h, ffn, (gate_up + down if l == 0 else down, l), seq, alpha, row(pre_g), row(pre_b),
            w_in[l].astype(BF16), row(b_in[l]), pool_w[l].astype(BF16), row(pool_scale[l]),
            row(sgu_ln_g[l]), row(sgu_ln_b[l]), sgu_w[l].astype(BF16), sgu_bias,
            p_a[l].astype(BF16), p_b[l].astype(BF16), w_out[l].astype(BF16),
            row(ln1_g[l]), row(ln1_b[l]), w_router[l].T.astype(BF16))
        experts = experts + cast_out
        idx, gates = _route(logits_t, n_batch, seq, cap)
        ffn, *experts = _moe(hs.reshape(n_batch, seq * rpt, LANES), idx, gates, *experts,
                             (gate_up, l + 1) if l + 1 < depth else None)
        ffn = ffn.reshape(t * rpt, LANES)
        pre_g, pre_b = ln2_g[l], ln2_b[l]
    out = _ln2(h, ffn, alpha, row(pre_g), row(pre_b))
    return out.reshape(n_batch, seq, d)
```

```python
import functools

import jax
import jax.numpy as jnp
from jax import lax
from jax.experimental import pallas as pl
from jax.experimental.pallas import tpu as pltpu

F32 = jnp.float32
BF16 = jnp.bfloat16

POOL_WINDOWS = (2, 4, 8, 16)
POOL_HALO = 8
SGU_CHUNK = 128
SGU_HEADS = 8
CAPACITY_FACTOR = 2
LN_EPS = 1e-5
LANES = 128
SUBLANES = 8
VMEM_LIMIT = 249 * 256 * 1024
MIX_TM = 512
LN_TM = 1024


def _ln(x, g, b):
    mu = jnp.mean(x, axis=-1, keepdims=True)
    xc = x - mu
    var = jnp.mean(xc * xc, axis=-1, keepdims=True)
    return xc * lax.rsqrt(var + LN_EPS) * g + b


def _dot(a, b):
    return jnp.dot(a, b, preferred_element_type=F32)


def _slab_rows(ref, n, rpt):
    return jnp.concatenate([ref[pl.ds(q, n, stride=rpt), :] for q in range(rpt)], axis=1)


def _cast_plan(stacked, layer, n_steps, step_of):
    ops, in_specs, out_specs, out_shapes = [], [], [], []
    for w, parts in stacked:
        n_l, n_e, r, c = w.shape
        rows = n_e * r // n_steps
        assert 0 < rows <= r and r % rows == 0, "a cast block must not straddle two experts"
        per_expert = r // rows
        ops.append(w.reshape(n_l * n_steps, rows, c))

        def in_map(*ids, _l=layer):
            return (_l * n_steps + step_of(*ids), 0, 0)

        def out_map(*ids, _per=per_expert):
            lin = step_of(*ids)
            return (lin // _per, 0, lin % _per, 0)

        in_specs.append(pl.BlockSpec((1, rows, c), in_map))
        out_specs.append(pl.BlockSpec((1, parts, rows, c // parts), out_map))
        out_shapes.append(jax.ShapeDtypeStruct((n_e, parts, r, c // parts), BF16))
    return ops, in_specs, out_specs, out_shapes


def _cast_blocks(src_refs, dst_refs):
    for s, o in zip(src_refs, dst_refs):
        width = o.shape[3]
        for p in range(o.shape[1]):
            o[0, p] = s[0, :, p * width:(p + 1) * width].astype(BF16)


def _mixer_kernel(has_ffn, n_cast, seq, alpha, *refs):
    if has_ffn:
        x_ref, xp_ref, xn_ref, f_ref, fp_ref, fn_ref = refs[:6]
        refs = refs[6:]
    else:
        x_ref, xp_ref, xn_ref = refs[:3]
        refs = refs[3:]
    cast_src, refs = refs[:n_cast], refs[n_cast:]
    n_fixed = 16 + 3
    (preg_ref, preb_ref, win_ref, bin_ref, poolw_ref, pscale_ref, sg_ref, sb_ref, sw_ref, sbias_ref,
     pa_ref, pb_ref, wout_ref, l1g_ref, l1b_ref, wrt_ref,
     h_ref, hs_ref, logit_ref) = refs[:n_fixed]
    cast_dst, refs = refs[n_fixed:n_fixed + n_cast], refs[n_fixed + n_cast:]
    proj_ref, aext_ref, ya_ref, yb_ref = refs
    _cast_blocks(cast_src, cast_dst)
    tm, d = x_ref.shape
    rpt = d // LANES
    pw = ya_ref.shape[1]
    gd = pw // len(POOL_WINDOWS)
    sw = yb_ref.shape[1]
    hd = sw // SGU_HEADS
    o_u, o_v, o_ga, o_gb = pw, pw + sw, pw + 2 * sw, pw + 2 * sw + d

    i = pl.program_id(0)
    tiles_per_seq = seq // tm
    pos = i % tiles_per_seq

    x = x_ref[...]
    xp = xp_ref[...]
    xn = xn_ref[...]
    if has_ffn:
        x = alpha * x + _slab_rows(f_ref, tm, rpt)
        xp = alpha * xp + _slab_rows(fp_ref, POOL_HALO, rpt)
        xn = alpha * xn + _slab_rows(fn_ref, POOL_HALO, rpt)
    x = _ln(x, preg_ref[...], preb_ref[...])
    xp = _ln(xp, preg_ref[...], preb_ref[...])
    xn = _ln(xn, preg_ref[...], preb_ref[...])
    xb = x.astype(BF16)

    proj_ref[...] = _dot(xb, win_ref[...]) + bin_ref[...]
    ap = _dot(xp.astype(BF16), win_ref[:, 0:pw]) + bin_ref[:, 0:pw]
    an = _dot(xn.astype(BF16), win_ref[:, 0:pw]) + bin_ref[:, 0:pw]
    ap = jnp.where(pos == 0, 0.0, ap)
    an = jnp.where(pos == tiles_per_seq - 1, 0.0, an)
    aext_ref[0:POOL_HALO, :] = ap
    aext_ref[POOL_HALO:POOL_HALO + tm, :] = proj_ref[:, 0:pw]
    aext_ref[POOL_HALO + tm:2 * POOL_HALO + tm, :] = an

    for c in range(tm // SGU_CHUNK):
        r0 = c * SGU_CHUNK
        rows = pl.ds(r0, SGU_CHUNK)
        s = pos * tm + r0 + lax.broadcasted_iota(jnp.int32, (SGU_CHUNK, 1), 0)
        for g, w in enumerate(POOL_WINDOWS):
            cols = slice(g * gd, (g + 1) * gd)
            acc = aext_ref[pl.ds(POOL_HALO + r0 - w // 2, SGU_CHUNK), cols]
            for o in range(-w // 2 + 1, w // 2):
                acc = acc + aext_ref[pl.ds(POOL_HALO + r0 + o, SGU_CHUNK), cols]
            cnt = (jnp.minimum(s + w // 2, seq) - jnp.maximum(s - w // 2, 0)).astype(F32)
            pooled = acc / cnt - proj_ref[rows, cols]
            ya = _dot(pooled.astype(BF16), poolw_ref[g]) * pscale_ref[:, cols]
            ya_ref[rows, cols] = ya.astype(BF16)
        gu = jax.nn.gelu(proj_ref[rows, o_u:o_v])
        gv = jax.nn.gelu(proj_ref[rows, o_v:o_ga])
        vb = _ln(gv, sg_ref[...], sb_ref[...]).astype(BF16)
        for hh in range(SGU_HEADS):
            hc = slice(hh * hd, (hh + 1) * hd)
            mixed = _dot(sw_ref[hh], vb[:, hc]) + sbias_ref[:, hc]
            yb_ref[rows, hc] = (gu[:, hc] * mixed).astype(BF16)

    ta = _dot(ya_ref[...], pa_ref[...])
    tb = _dot(yb_ref[...], pb_ref[...])
    merged = (jax.nn.sigmoid(proj_ref[:, o_ga:o_gb]) * ta
              + jax.nn.sigmoid(proj_ref[:, o_gb:o_gb + d]) * tb)
    mix = _dot(merged.astype(BF16), wout_ref[...])
    h1 = _ln(alpha * x + mix, l1g_ref[...], l1b_ref[...])
    h_ref[...] = h1
    hb = h1.astype(BF16)
    logit_ref[...] = lax.dot_general(wrt_ref[...], hb, (((1,), (1,)), ((), ())), preferred_element_type=F32)

    for q in range(rpt):
        hs_ref[pl.ds(q, tm, stride=rpt), :] = h1[:, q * LANES:(q + 1) * LANES]


def _mixer(x2d, ffn_slab, cast, seq, alpha, pre_g, pre_b, w_in, b_in, pool_w, pool_scale, sgu_g, sgu_b, sgu_w,
           sgu_bias, p_a, p_b, w_out, ln1_g, ln1_b, w_rt):
    t, d = x2d.shape
    tm = MIX_TM
    rpt = d // LANES
    n_e = w_rt.shape[0]
    pw = p_a.shape[0]
    sw = p_b.shape[0]
    hb = tm // POOL_HALO
    n_hb = t // POOL_HALO
    has_ffn = ffn_slab is not None

    def full(a):
        nd = a.ndim
        return pl.BlockSpec(a.shape, lambda i, _n=nd: (0,) * _n)

    def prev_halo(i):
        return (jnp.maximum(i * hb - 1, 0), 0)

    def next_halo(i):
        return (jnp.minimum((i + 1) * hb, n_hb - 1), 0)

    acts = [x2d, x2d, x2d]
    act_specs = [
        pl.BlockSpec((tm, d), lambda i: (i, 0)),
        pl.BlockSpec((POOL_HALO, d), prev_halo),
        pl.BlockSpec((POOL_HALO, d), next_halo),
    ]
    if has_ffn:
        acts += [ffn_slab, ffn_slab, ffn_slab]
        act_specs += [
            pl.BlockSpec((tm * rpt, LANES), lambda i: (i, 0)),
            pl.BlockSpec((POOL_HALO * rpt, LANES), prev_halo),
            pl.BlockSpec((POOL_HALO * rpt, LANES), next_halo),
        ]
    weights = (pre_g, pre_b, w_in, b_in, pool_w, pool_scale, sgu_g, sgu_b, sgu_w, sgu_bias, p_a, p_b, w_out,
               ln1_g, ln1_b, w_rt)
    c_ops, c_in, c_out, c_shapes = ([], [], [], []) if cast is None else _cast_plan(
        cast[0], cast[1], t // tm, lambda i: i)
    return pl.pallas_call(
        functools.partial(_mixer_kernel, has_ffn, len(c_ops), seq, alpha),
        grid=(t // tm,),
        in_specs=act_specs + c_in + [full(a) for a in weights],
        out_specs=[
            pl.BlockSpec((tm, d), lambda i: (i, 0)),
            pl.BlockSpec((tm * rpt, LANES), lambda i: (i, 0)),
            pl.BlockSpec((n_e, tm), lambda i: (0, i)),
        ] + c_out,
        out_shape=[
            jax.ShapeDtypeStruct((t, d), F32),
            jax.ShapeDtypeStruct((t * rpt, LANES), F32),
            jax.ShapeDtypeStruct((n_e, t), F32),
        ] + c_shapes,
        scratch_shapes=[
            pltpu.VMEM((tm, w_in.shape[1]), F32),
            pltpu.VMEM((tm + 2 * POOL_HALO, pw), F32),
            pltpu.VMEM((tm, pw), BF16),
            pltpu.VMEM((tm, sw), BF16),
        ],
        compiler_params=pltpu.CompilerParams(
            dimension_semantics=("arbitrary",), vmem_limit_bytes=VMEM_LIMIT),
        name="mixer",
    )(*acts, *c_ops, *weights)


def _route_kernel(cap, logit_ref, idx_ref, gate_ref):
    n_e, seq = logit_ref.shape
    n_tiles = seq // LANES

    lg = logit_ref[...]
    ex = jnp.exp(lg - jnp.max(lg, axis=0, keepdims=True))
    aff = ex / jnp.sum(ex, axis=0, keepdims=True)

    def bit_step(k, thr):
        cand = thr | jnp.left_shift(jnp.int32(1), 30 - k)
        cand_f = pltpu.bitcast(cand, F32)
        cnt = jnp.sum(jnp.where(aff >= cand_f, 1.0, 0.0), axis=1, keepdims=True)
        return jnp.where(cnt >= cap, cand, thr)

    thr = lax.fori_loop(0, 31, bit_step, jnp.zeros((n_e, 1), jnp.int32))
    thr_f = pltpu.bitcast(thr, F32)
    gt = aff > thr_f
    eq = aff == thr_f
    need = cap - jnp.sum(jnp.where(gt, 1.0, 0.0), axis=1, keepdims=True)

    tri = (lax.broadcasted_iota(jnp.int32, (LANES, LANES), 0)
           <= lax.broadcasted_iota(jnp.int32, (LANES, LANES), 1)).astype(BF16)
    carry_gt = jnp.zeros((n_e, 1), F32)
    carry_eq = jnp.zeros((n_e, 1), F32)
    ranks = []
    for k in range(n_tiles):
        cols = slice(k * LANES, (k + 1) * LANES)
        gt_k = gt[:, cols]
        eq_k = eq[:, cols]
        c_gt = _dot(jnp.where(gt_k, 1.0, 0.0).astype(BF16), tri) + carry_gt
        c_eq = _dot(jnp.where(eq_k, 1.0, 0.0).astype(BF16), tri) + carry_eq
        carry_gt = c_gt[:, LANES - 1:LANES]
        carry_eq = c_eq[:, LANES - 1:LANES]
        sel_k = gt_k | (eq_k & (c_eq <= need))
        rank = c_gt + jnp.minimum(c_eq, need)
        ranks.append(jnp.where(sel_k, rank, 0.0))
    rank_all = jnp.concatenate(ranks, axis=1).astype(jnp.int32)

    pos_bits = seq.bit_length() - 1
    tok = lax.broadcasted_iota(jnp.int32, (n_e, seq), 1)
    key = jnp.where(rank_all > 0, tok | jnp.left_shift(tok - (rank_all - 1), pos_bits), 0)
    gate = aff
    for s in range(pos_bits):
        bit = jnp.int32(1 << (pos_bits + s))
        key_in = pltpu.roll(key, seq - (1 << s), 1)
        gate_in = pltpu.roll(gate, seq - (1 << s), 1)
        arriving = (key_in & bit) != 0
        leaving = (key & bit) != 0
        key = jnp.where(arriving, key_in, jnp.where(leaving, 0, key))
        gate = jnp.where(arriving, gate_in, gate)
    idx_ref[0] = key[:, :cap] & jnp.int32(seq - 1)
    gate_ref[0] = gate[:, :cap]


def _route(logits_t, n_batch, seq, cap):
    n_e = logits_t.shape[0]
    return pl.pallas_call(
        functools.partial(_route_kernel, cap),
        grid=(n_batch,),
        in_specs=[pl.BlockSpec((n_e, seq), lambda b: (0, b))],
        out_specs=[
            pl.BlockSpec((1, n_e, cap), lambda b: (b, 0, 0)),
            pl.BlockSpec((1, n_e, cap), lambda b: (b, 0, 0)),
        ],
        out_shape=[
            jax.ShapeDtypeStruct((n_batch, n_e, cap), jnp.int32),
            jax.ShapeDtypeStruct((n_batch, n_e, cap), F32),
        ],
        compiler_params=pltpu.CompilerParams(
            dimension_semantics=("arbitrary",), vmem_limit_bytes=VMEM_LIMIT),
        name="route",
    )(logits_t)


def _moe_kernel(row_stride, n_cast, idx_ref, idxp_ref, idxn_ref, gate_ref, gatep_ref, hs_hbm, wg_hbm, wu_hbm,
                wd_hbm, *refs):
    cast_src, refs = refs[:n_cast], refs[n_cast:]
    out_ref = refs[0]
    cast_dst = refs[1:1 + n_cast]
    xe_ref, xt_ref, yt_ref, hs_ref, wg_buf, wu_buf, wd_buf, sem, hs_sem = refs[1 + n_cast:]
    cap, d = xe_ref.shape
    fc = wg_buf.shape[2]
    rpt = d // LANES
    b = pl.program_id(0)
    e = pl.program_id(1)
    n_b = pl.num_programs(0)
    n_e = pl.num_programs(1)
    group = 8

    def weight_copies(expert, half):
        return (pltpu.make_async_copy(wg_hbm.at[expert, half], wg_buf.at[half], sem.at[half, 0]),
                pltpu.make_async_copy(wu_hbm.at[expert, half], wu_buf.at[half], sem.at[half, 1]),
                pltpu.make_async_copy(wd_hbm.at[expert, 0, pl.ds(half * fc, fc)], wd_buf.at[half],
                                      sem.at[half, 2]))

    def rows_copy(sequence):
        return pltpu.make_async_copy(hs_hbm.at[sequence], hs_ref, hs_sem)

    def gather_row(src_idx_ref, j):
        r = pl.multiple_of(src_idx_ref[0, 0, j] * rpt, rpt)
        xt_ref[pl.ds(j, rpt, stride=row_stride), :] = hs_ref[pl.ds(r, rpt), :]

    def scatter_rows(src_idx_ref, src_gate_ref, js):
        rows = [pl.multiple_of(src_idx_ref[0, 0, j] * rpt, rpt) for j in js]
        vals = [out_ref[0, pl.ds(r, rpt), :] + yt_ref[pl.ds(j, rpt, stride=row_stride), :] * src_gate_ref[0, 0, j]
                for r, j in zip(rows, js)]
        for r, v in zip(rows, vals):
            out_ref[0, pl.ds(r, rpt), :] = v

    def ffn_half(half):
        x = xe_ref[...]
        g = _dot(x, wg_buf[half])
        u = _dot(x, wu_buf[half])
        hid = (jax.nn.silu(g) * u).astype(BF16)
        return _dot(hid, wd_buf[half])

    first_step = (b == 0) & (e == 0)
    last_step = (b == n_b - 1) & (e == n_e - 1)

    @pl.when(first_step)
    def _():
        for c in weight_copies(e, 0):
            c.start()
        rows_copy(b).start()

    for c in weight_copies(e, 1):
        c.start()

    @pl.when((e == n_e - 1) & (b < n_b - 1))
    def _():
        rows_copy(b + 1).start()

    @pl.when(e == 0)
    def _():
        out_ref[...] = jnp.zeros_like(out_ref)
        yt_ref[...] = jnp.zeros_like(yt_ref)
        rows_copy(b).wait()

        def gather(jj, _):
            for u in range(group):
                gather_row(idx_ref, jj * group + u)
            return 0

        lax.fori_loop(0, cap // group, gather, 0)

    for c in weight_copies(e, 0):
        c.wait()

    _cast_blocks(cast_src, cast_dst)
    for q in range(rpt):
        xe_ref[:, q * LANES:(q + 1) * LANES] = xt_ref[pl.ds(q * row_stride, cap), :].astype(BF16)
    for j0 in range(0, cap, group):
        scatter_rows(idxp_ref, gatep_ref, range(j0, j0 + group))
    y = ffn_half(0)
    for q in range(rpt):
        yt_ref[pl.ds(q * row_stride, cap), :] = y[:, q * LANES:(q + 1) * LANES]

    @pl.when(jnp.logical_not(last_step))
    def _():
        for c in weight_copies(jnp.where(e + 1 < n_e, e + 1, 0), 0):
            c.start()

    for c in weight_copies(e, 1):
        c.wait()

    def second_half(gather_next):
        if gather_next:
            for j in range(cap):
                gather_row(idxn_ref, j)
        y = ffn_half(1)
        for q in range(rpt):
            yt_ref[pl.ds(q * row_stride, cap), :] += y[:, q * LANES:(q + 1) * LANES]

    @pl.when(e < n_e - 1)
    def _():
        second_half(True)

    @pl.when(e == n_e - 1)
    def _():
        second_half(False)

        def scatter(jj, _):
            scatter_rows(idx_ref, gate_ref, [jj * group + u for u in range(group)])
            return 0

        lax.fori_loop(0, cap // group, scatter, 0)


def _moe(hs3, idx, gates, w_gate, w_up, w_down, cast):
    n_batch, srows, _ = hs3.shape
    n_e, n_f, d, fc = w_gate.shape
    assert n_f == 2
    cap = idx.shape[-1]
    rpt = d // LANES
    row_stride = cap + SUBLANES
    n_be = n_batch * n_e
    idx3 = idx.reshape(n_be, 1, cap)
    gates3 = jnp.concatenate([gates.reshape(n_be, 1, cap), jnp.zeros((1, 1, cap), F32)], axis=0)

    def smem(index_map):
        return pl.BlockSpec((1, 1, cap), index_map, memory_space=pltpu.SMEM)

    hbm = pl.BlockSpec(memory_space=pl.ANY)
    c_ops, c_in, c_out, c_shapes = ([], [], [], []) if cast is None else _cast_plan(
        cast[0], cast[1], n_be, lambda b, e: b * n_e + e)
    return pl.pallas_call(
        functools.partial(_moe_kernel, row_stride, len(c_ops)),
        grid=(n_batch, n_e),
        in_specs=[
            smem(lambda b, e: (b * n_e + e, 0, 0)),
            smem(lambda b, e: (jnp.maximum(b * n_e + e - 1, 0), 0, 0)),
            smem(lambda b, e: (jnp.minimum(b * n_e + e + 1, n_be - 1), 0, 0)),
            smem(lambda b, e: (b * n_e + e, 0, 0)),
            smem(lambda b, e: (jnp.where(e == 0, n_be, b * n_e + e - 1), 0, 0)),
            hbm, hbm, hbm, hbm,
        ] + c_in,
        out_specs=[pl.BlockSpec((1, srows, LANES), lambda b, e: (b, 0, 0), pipeline_mode=pl.Buffered(1))]
        + c_out,
        out_shape=[jax.ShapeDtypeStruct((n_batch, srows, LANES), F32)] + c_shapes,
        scratch_shapes=[
            pltpu.VMEM((cap, d), BF16),
            pltpu.VMEM((rpt * row_stride, LANES), F32),
            pltpu.VMEM((rpt * row_stride, LANES), F32),
            pltpu.VMEM((srows, LANES), F32),
            pltpu.VMEM((2, d, fc), BF16),
            pltpu.VMEM((2, d, fc), BF16),
            pltpu.VMEM((2, fc, d), BF16),
            pltpu.SemaphoreType.DMA((2, 3)),
            pltpu.SemaphoreType.DMA(()),
        ],
        compiler_params=pltpu.CompilerParams(
            dimension_semantics=("arbitrary", "arbitrary"), vmem_limit_bytes=VMEM_LIMIT),
        name="moe",
    )(idx3, idx3, idx3, gates3, gates3, hs3, w_gate, w_up, w_down, *c_ops)


def _ln2_kernel(alpha, h_ref, f_ref, g_ref, b_ref, o_ref):
    tm, d = h_ref.shape
    o_ref[...] = _ln(alpha * h_ref[...] + _slab_rows(f_ref, tm, d // LANES), g_ref[...], b_ref[...])


def _ln2(h2d, ffn_slab, alpha, g, b):
    t, d = h2d.shape
    tm = LN_TM
    rpt = d // LANES
    return pl.pallas_call(
        functools.partial(_ln2_kernel, alpha),
        grid=(t // tm,),
        in_specs=[
            pl.BlockSpec((tm, d), lambda i: (i, 0)),
            pl.BlockSpec((tm * rpt, LANES), lambda i: (i, 0)),
            pl.BlockSpec((1, d), lambda i: (0, 0)),
            pl.BlockSpec((1, d), lambda i: (0, 0)),
        ],
        out_specs=pl.BlockSpec((tm, d), lambda i: (i, 0)),
        out_shape=jax.ShapeDtypeStruct((t, d), F32),
        compiler_params=pltpu.CompilerParams(
            dimension_semantics=("arbitrary",), vmem_limit_bytes=VMEM_LIMIT),
        name="ln2",
    )(h2d, ffn_slab, g, b)


def kernel(x, in_ln_g, in_ln_b, w_in, b_in, pool_w, pool_scale, sgu_ln_g, sgu_ln_b, sgu_w, sgu_b, p_a, p_b,
           w_out, ln1_g, ln1_b, w_router, w_gate, w_up, w_down, ln2_g, ln2_b):
    n_batch, seq, d = x.shape
    depth = w_in.shape[0]
    n_e = w_router.shape[-1]
    cap = CAPACITY_FACTOR * seq // n_e
    alpha = (2 * depth) ** 0.25
    t = n_batch * seq
    hd = sgu_ln_g.shape[-1] // SGU_HEADS
    rpt = d // LANES

    def row(a):
        return a.reshape(1, -1)

    gate_up = ((w_gate, 2), (w_up, 2))
    down = ((w_down, 1),)
    h = x.reshape(t, d)
    ffn = None
    experts = []
    pre_g, pre_b = in_ln_g, in_ln_b
    for l in range(depth):
        sgu_bias = jnp.repeat(sgu_b[l].T, hd, axis=1)
        h, hs, logits_t, *cast_out = _mixer(
            h, ffn, (gate_up + down if l == 0 else down, l), seq, alpha, row(pre_g), row(pre_b),
            w_in[l].astype(BF16), row(b_in[l]), pool_w[l].astype(BF16), row(pool_scale[l]),
            row(sgu_ln_g[l]), row(sgu_ln_b[l]), sgu_w[l].astype(BF16), sgu_bias,
            p_a[l].astype(BF16), p_b[l].astype(BF16), w_out[l].astype(BF16),
            row(ln1_g[l]), row(ln1_b[l]), w_router[l].T.astype(BF16))
        experts = experts + cast_out
        idx, gates = _route(logits_t, n_batch, seq, cap)
        ffn, *experts = _moe(hs.reshape(n_batch, seq * rpt, LANES), idx, gates, *experts,
                             (gate_up, l + 1) if l + 1 < depth else None)
        ffn = ffn.reshape(t * rpt, LANES)
        pre_g, pre_b = ln2_g[l], ln2_b[l]
    out = _ln2(h, ffn, alpha, row(pre_g), row(pre_b))
    return out.reshape(n_batch, seq, d)
```

```python
import functools

import jax
import jax.numpy as jnp
from jax import lax
from jax.experimental import pallas as pl
from jax.experimental.pallas import tpu as pltpu

F32 = jnp.float32
BF16 = jnp.bfloat16

POOL_WINDOWS = (2, 4, 8, 16)
POOL_HALO = 8
SGU_CHUNK = 128
SGU_HEADS = 8
CAPACITY_FACTOR = 2
LN_EPS = 1e-5
LANES = 128
SUBLANES = 8
VMEM_LIMIT = 249 * 256 * 1024
MIX_TM = 512
LN_TM = 1024


def _ln(x, g, b):
    mu = jnp.mean(x, axis=-1, keepdims=True)
    xc = x - mu
    var = jnp.mean(xc * xc, axis=-1, keepdims=True)
    return xc * lax.rsqrt(var + LN_EPS) * g + b


def _dot(a, b):
    return jnp.dot(a, b, preferred_element_type=F32)


def _slab_rows(ref, n, rpt):
    return jnp.concatenate([ref[pl.ds(q, n, stride=rpt), :] for q in range(rpt)], axis=1)


def _cast_plan(stacked, layer, n_steps, step_of):
    ops, in_specs, out_specs, out_shapes = [], [], [], []
    for w, parts in stacked:
        n_l, n_e, r, c = w.shape
        rows = n_e * r // n_steps
        assert 0 < rows <= r and r % rows == 0, "a cast block must not straddle two experts"
        per_expert = r // rows
        ops.append(w.reshape(n_l * n_steps, rows, c))

        def in_map(*ids, _l=layer):
            return (_l * n_steps + step_of(*ids), 0, 0)

        def out_map(*ids, _per=per_expert):
            lin = step_of(*ids)
            return (lin // _per, 0, lin % _per, 0)

        in_specs.append(pl.BlockSpec((1, rows, c), in_map))
        out_specs.append(pl.BlockSpec((1, parts, rows, c // parts), out_map))
        out_shapes.append(jax.ShapeDtypeStruct((n_e, parts, r, c // parts), BF16))
    return ops, in_specs, out_specs, out_shapes


def _cast_blocks(src_refs, dst_refs):
    for s, o in zip(src_refs, dst_refs):
        width = o.shape[3]
        for p in range(o.shape[1]):
            o[0, p] = s[0, :, p * width:(p + 1) * width].astype(BF16)


def _mixer_kernel(has_ffn, n_cast, seq, alpha, *refs):
    if has_ffn:
        x_ref, xp_ref, xn_ref, f_ref, fp_ref, fn_ref = refs[:6]
        refs = refs[6:]
    else:
        x_ref, xp_ref, xn_ref = refs[:3]
        refs = refs[3:]
    cast_src, refs = refs[:n_cast], refs[n_cast:]
    n_fixed = 16 + 3
    (preg_ref, preb_ref, win_ref, bin_ref, poolw_ref, pscale_ref, sg_ref, sb_ref, sw_ref, sbias_ref,
     pa_ref, pb_ref, wout_ref, l1g_ref, l1b_ref, wrt_ref,
     h_ref, hs_ref, logit_ref) = refs[:n_fixed]
    cast_dst, refs = refs[n_fixed:n_fixed + n_cast], refs[n_fixed + n_cast:]
    proj_ref, aext_ref, ya_ref, yb_ref = refs
    _cast_blocks(cast_src, cast_dst)
    tm, d = x_ref.shape
    rpt = d // LANES
    pw = ya_ref.shape[1]
    gd = pw // len(POOL_WINDOWS)
    sw = yb_ref.shape[1]
    hd = sw // SGU_HEADS
    o_u, o_v, o_ga, o_gb = pw, pw + sw, pw + 2 * sw, pw + 2 * sw + d

    i = pl.program_id(0)
    tiles_per_seq = seq // tm
    pos = i % tiles_per_seq

    x = x_ref[...]
    xp = xp_ref[...]
    xn = xn_ref[...]
    if has_ffn:
        x = alpha * x + _slab_rows(f_ref, tm, rpt)
        xp = alpha * xp + _slab_rows(fp_ref, POOL_HALO, rpt)
        xn = alpha * xn + _slab_rows(fn_ref, POOL_HALO, rpt)
    x = _ln(x, preg_ref[...], preb_ref[...])
    xp = _ln(xp, preg_ref[...], preb_ref[...])
    xn = _ln(xn, preg_ref[...], preb_ref[...])
    xb = x.astype(BF16)

    proj_ref[...] = _dot(xb, win_ref[...]) + bin_ref[...]
    ap = _dot(xp.astype(BF16), win_ref[:, 0:pw]) + bin_ref[:, 0:pw]
    an = _dot(xn.astype(BF16), win_ref[:, 0:pw]) + bin_ref[:, 0:pw]
    ap = jnp.where(pos == 0, 0.0, ap)
    an = jnp.where(pos == tiles_per_seq - 1, 0.0, an)
    aext_ref[0:POOL_HALO, :] = ap
    aext_ref[POOL_HALO:POOL_HALO + tm, :] = proj_ref[:, 0:pw]
    aext_ref[POOL_HALO + tm:2 * POOL_HALO + tm, :] = an

    for c in range(tm // SGU_CHUNK):
        r0 = c * SGU_CHUNK
        rows = pl.ds(r0, SGU_CHUNK)
        s = pos * tm + r0 + lax.broadcasted_iota(jnp.int32, (SGU_CHUNK, 1), 0)
        for g, w in enumerate(POOL_WINDOWS):
            cols = slice(g * gd, (g + 1) * gd)
            acc = aext_ref[pl.ds(POOL_HALO + r0 - w // 2, SGU_CHUNK), cols]
            for o in range(-w // 2 + 1, w // 2):
                acc = acc + aext_ref[pl.ds(POOL_HALO + r0 + o, SGU_CHUNK), cols]
            cnt = (jnp.minimum(s + w // 2, seq) - jnp.maximum(s - w // 2, 0)).astype(F32)
            pooled = acc / cnt - proj_ref[rows, cols]
            ya = _dot(pooled.astype(BF16), poolw_ref[g]) * pscale_ref[:, cols]
            ya_ref[rows, cols] = ya.astype(BF16)
        gu = jax.nn.gelu(proj_ref[rows, o_u:o_v])
        gv = jax.nn.gelu(proj_ref[rows, o_v:o_ga])
        vb = _ln(gv, sg_ref[...], sb_ref[...]).astype(BF16)
        for hh in range(SGU_HEADS):
            hc = slice(hh * hd, (hh + 1) * hd)
            mixed = _dot(sw_ref[hh], vb[:, hc]) + sbias_ref[:, hc]
            yb_ref[rows, hc] = (gu[:, hc] * mixed).astype(BF16)

    ta = _dot(ya_ref[...], pa_ref[...])
    tb = _dot(yb_ref[...], pb_ref[...])
    merged = (jax.nn.sigmoid(proj_ref[:, o_ga:o_gb]) * ta
              + jax.nn.sigmoid(proj_ref[:, o_gb:o_gb + d]) * tb)
    mix = _dot(merged.astype(BF16), wout_ref[...])
    h1 = _ln(alpha * x + mix, l1g_ref[...], l1b_ref[...])
    h_ref[...] = h1
    hb = h1.astype(BF16)
    logit_ref[...] = lax.dot_general(wrt_ref[...], hb, (((1,), (1,)), ((), ())), preferred_element_type=F32)

    for q in range(rpt):
        hs_ref[pl.ds(q, tm, stride=rpt), :] = h1[:, q * LANES:(q + 1) * LANES]


def _mixer(x2d, ffn_slab, cast, seq, alpha, pre_g, pre_b, w_in, b_in, pool_w, pool_scale, sgu_g, sgu_b, sgu_w,
           sgu_bias, p_a, p_b, w_out, ln1_g, ln1_b, w_rt):
    t, d = x2d.shape
    tm = MIX_TM
    rpt = d // LANES
    n_e = w_rt.shape[0]
    pw = p_a.shape[0]
    sw = p_b.shape[0]
    hb = tm // POOL_HALO
    n_hb = t // POOL_HALO
    has_ffn = ffn_slab is not None

    def full(a):
        nd = a.ndim
        return pl.BlockSpec(a.shape, lambda i, _n=nd: (0,) * _n)

    def prev_halo(i):
        return (jnp.maximum(i * hb - 1, 0), 0)

    def next_halo(i):
        return (jnp.minimum((i + 1) * hb, n_hb - 1), 0)

    acts = [x2d, x2d, x2d]
    act_specs = [
        pl.BlockSpec((tm, d), lambda i: (i, 0)),
        pl.BlockSpec((POOL_HALO, d), prev_halo),
        pl.BlockSpec((POOL_HALO, d), next_halo),
    ]
    if has_ffn:
        acts += [ffn_slab, ffn_slab, ffn_slab]
        act_specs += [
            pl.BlockSpec((tm * rpt, LANES), lambda i: (i, 0)),
            pl.BlockSpec((POOL_HALO * rpt, LANES), prev_halo),
            pl.BlockSpec((POOL_HALO * rpt, LANES), next_halo),
        ]
    weights = (pre_g, pre_b, w_in, b_in, pool_w, pool_scale, sgu_g, sgu_b, sgu_w, sgu_bias, p_a, p_b, w_out,
               ln1_g, ln1_b, w_rt)
    c_ops, c_in, c_out, c_shapes = ([], [], [], []) if cast is None else _cast_plan(
        cast[0], cast[1], t // tm, lambda i: i)
    return pl.pallas_call(
        functools.partial(_mixer_kernel, has_ffn, len(c_ops), seq, alpha),
        grid=(t // tm,),
        in_specs=act_specs + c_in + [full(a) for a in weights],
        out_specs=[
            pl.BlockSpec((tm, d), lambda i: (i, 0)),
            pl.BlockSpec((tm * rpt, LANES), lambda i: (i, 0)),
            pl.BlockSpec((n_e, tm), lambda i: (0, i)),
        ] + c_out,
        out_shape=[
            jax.ShapeDtypeStruct((t, d), F32),
            jax.ShapeDtypeStruct((t * rpt, LANES), F32),
            jax.ShapeDtypeStruct((n_e, t), F32),
        ] + c_shapes,
        scratch_shapes=[
            pltpu.VMEM((tm, w_in.shape[1]), F32),
            pltpu.VMEM((tm + 2 * POOL_HALO, pw), F32),
            pltpu.VMEM((tm, pw), BF16),
            pltpu.VMEM((tm, sw), BF16),
        ],
        compiler_params=pltpu.CompilerParams(
            dimension_semantics=("arbitrary",), vmem_limit_bytes=VMEM_LIMIT),
        name="mixer",
    )(*acts, *c_ops, *weights)


def _route_kernel(cap, logit_ref, idx_ref, gate_ref):
    n_e, seq = logit_ref.shape
    n_tiles = seq // LANES

    lg = logit_ref[...]
    ex = jnp.exp(lg - jnp.max(lg, axis=0, keepdims=True))
    aff = ex / jnp.sum(ex, axis=0, keepdims=True)

    def enough(cand):
        return jnp.sum(jnp.where(aff >= pltpu.bitcast(cand, F32), 1.0, 0.0), axis=1, keepdims=True) >= cap

    def bit_step(k, thr):
        hi = jnp.left_shift(jnp.int32(1), 29 - 2 * k)
        lo = jnp.left_shift(jnp.int32(1), 28 - 2 * k)
        return jnp.where(enough(thr | hi | lo), thr | hi | lo,
                         jnp.where(enough(thr | hi), thr | hi, jnp.where(enough(thr | lo), thr | lo, thr)))

    thr = lax.fori_loop(0, 15, bit_step, jnp.zeros((n_e, 1), jnp.int32))
    thr_f = pltpu.bitcast(thr, F32)
    gt = aff > thr_f
    eq = aff == thr_f
    need = cap - jnp.sum(jnp.where(gt, 1.0, 0.0), axis=1, keepdims=True)

    tri = (lax.broadcasted_iota(jnp.int32, (LANES, LANES), 0)
           <= lax.broadcasted_iota(jnp.int32, (LANES, LANES), 1)).astype(BF16)
    carry_gt = jnp.zeros((n_e, 1), F32)
    carry_eq = jnp.zeros((n_e, 1), F32)
    ranks = []
    for k in range(n_tiles):
        cols = slice(k * LANES, (k + 1) * LANES)
        gt_k = gt[:, cols]
        eq_k = eq[:, cols]
        c_gt = _dot(jnp.where(gt_k, 1.0, 0.0).astype(BF16), tri) + carry_gt
        c_eq = _dot(jnp.where(eq_k, 1.0, 0.0).astype(BF16), tri) + carry_eq
        carry_gt = c_gt[:, LANES - 1:LANES]
        carry_eq = c_eq[:, LANES - 1:LANES]
        sel_k = gt_k | (eq_k & (c_eq <= need))
        rank = c_gt + jnp.minimum(c_eq, need)
        ranks.append(jnp.where(sel_k, rank, 0.0))
    rank_all = jnp.concatenate(ranks, axis=1).astype(jnp.int32)

    pos_bits = seq.bit_length() - 1
    tok = lax.broadcasted_iota(jnp.int32, (n_e, seq), 1)
    key = jnp.where(rank_all > 0, tok | jnp.left_shift(tok - (rank_all - 1), pos_bits), 0)
    gate = aff
    for s in range(pos_bits):
        bit = jnp.int32(1 << (pos_bits + s))
        key_in = pltpu.roll(key, seq - (1 << s), 1)
        gate_in = pltpu.roll(gate, seq - (1 << s), 1)
        arriving = (key_in & bit) != 0
        leaving = (key & bit) != 0
        key = jnp.where(arriving, key_in, jnp.where(leaving, 0, key))
        gate = jnp.where(arriving, gate_in, gate)
    idx_ref[0] = key[:, :cap] & jnp.int32(seq - 1)
    gate_ref[0] = gate[:, :cap]


def _route(logits_t, n_batch, seq, cap):
    n_e = logits_t.shape[0]
    return pl.pallas_call(
        functools.partial(_route_kernel, cap),
        grid=(n_batch,),
        in_specs=[pl.BlockSpec((n_e, seq), lambda b: (0, b))],
        out_specs=[
            pl.BlockSpec((1, n_e, cap), lambda b: (b, 0, 0)),
            pl.BlockSpec((1, n_e, cap), lambda b: (b, 0, 0)),
        ],
        out_shape=[
            jax.ShapeDtypeStruct((n_batch, n_e, cap), jnp.int32),
            jax.ShapeDtypeStruct((n_batch, n_e, cap), F32),
        ],
        compiler_params=pltpu.CompilerParams(
            dimension_semantics=("arbitrary",), vmem_limit_bytes=VMEM_LIMIT),
        name="route",
    )(logits_t)


def _moe_kernel(row_stride, n_cast, idx_ref, idxp_ref, idxn_ref, gate_ref, gatep_ref, hs_hbm, wg_hbm, wu_hbm,
                wd_hbm, *refs):
    cast_src, refs = refs[:n_cast], refs[n_cast:]
    out_ref = refs[0]
    cast_dst = refs[1:1 + n_cast]
    xe_ref, xt_ref, yt_ref, hs_ref, wg_buf, wu_buf, wd_buf, sem, hs_sem = refs[1 + n_cast:]
    cap, d = xe_ref.shape
    fc = wg_buf.shape[2]
    rpt = d // LANES
    b = pl.program_id(0)
    e = pl.program_id(1)
    n_b = pl.num_programs(0)
    n_e = pl.num_programs(1)
    group = 8

    def weight_copies(expert, half):
        return (pltpu.make_async_copy(wg_hbm.at[expert, half], wg_buf.at[half], sem.at[half, 0]),
                pltpu.make_async_copy(wu_hbm.at[expert, half], wu_buf.at[half], sem.at[half, 1]),
                pltpu.make_async_copy(wd_hbm.at[expert, 0, pl.ds(half * fc, fc)], wd_buf.at[half],
                                      sem.at[half, 2]))

    def rows_copy(sequence):
        return pltpu.make_async_copy(hs_hbm.at[sequence], hs_ref, hs_sem)

    def gather_row(src_idx_ref, j):
        r = pl.multiple_of(src_idx_ref[0, 0, j] * rpt, rpt)
        xt_ref[pl.ds(j, rpt, stride=row_stride), :] = hs_ref[pl.ds(r, rpt), :]

    def scatter_rows(src_idx_ref, src_gate_ref, js):
        rows = [pl.multiple_of(src_idx_ref[0, 0, j] * rpt, rpt) for j in js]
        vals = [out_ref[0, pl.ds(r, rpt), :] + yt_ref[pl.ds(j, rpt, stride=row_stride), :] * src_gate_ref[0, 0, j]
                for r, j in zip(rows, js)]
        for r, v in zip(rows, vals):
            out_ref[0, pl.ds(r, rpt), :] = v

    def ffn_half(half):
        x = xe_ref[...]
        g = _dot(x, wg_buf[half])
        u = _dot(x, wu_buf[half])
        hid = (jax.nn.silu(g) * u).astype(BF16)
        return _dot(hid, wd_buf[half])

    first_step = (b == 0) & (e == 0)
    last_step = (b == n_b - 1) & (e == n_e - 1)

    @pl.when(first_step)
    def _():
        for c in weight_copies(e, 0):
            c.start()
        rows_copy(b).start()

    for c in weight_copies(e, 1):
        c.start()

    @pl.when((e == n_e - 1) & (b < n_b - 1))
    def _():
        rows_copy(b + 1).start()

    @pl.when(e == 0)
    def _():
        out_ref[...] = jnp.zeros_like(out_ref)
        yt_ref[...] = jnp.zeros_like(yt_ref)
        rows_copy(b).wait()

        def gather(jj, _):
            for u in range(group):
                gather_row(idx_ref, jj * group + u)
            return 0

        lax.fori_loop(0, cap // group, gather, 0)

    for c in weight_copies(e, 0):
        c.wait()

    _cast_blocks(cast_src, cast_dst)
    for q in range(rpt):
        xe_ref[:, q * LANES:(q + 1) * LANES] = xt_ref[pl.ds(q * row_stride, cap), :].astype(BF16)
    for j0 in range(0, cap, group):
        scatter_rows(idxp_ref, gatep_ref, range(j0, j0 + group))
    y = ffn_half(0)
    for q in range(rpt):
        yt_ref[pl.ds(q * row_stride, cap), :] = y[:, q * LANES:(q + 1) * LANES]

    @pl.when(jnp.logical_not(last_step))
    def _():
        for c in weight_copies(jnp.where(e + 1 < n_e, e + 1, 0), 0):
            c.start()

    for c in weight_copies(e, 1):
        c.wait()

    def second_half(gather_next):
        if gather_next:
            for j in range(cap):
                gather_row(idxn_ref, j)
        y = ffn_half(1)
        for q in range(rpt):
            yt_ref[pl.ds(q * row_stride, cap), :] += y[:, q * LANES:(q + 1) * LANES]

    @pl.when(e < n_e - 1)
    def _():
        second_half(True)

    @pl.when(e == n_e - 1)
    def _():
        second_half(False)

        def scatter(jj, _):
            scatter_rows(idx_ref, gate_ref, [jj * group + u for u in range(group)])
            return 0

        lax.fori_loop(0, cap // group, scatter, 0)


def _moe(hs3, idx, gates, w_gate, w_up, w_down, cast):
    n_batch, srows, _ = hs3.shape
    n_e, n_f, d, fc = w_gate.shape
    assert n_f == 2
    cap = idx.shape[-1]
    rpt = d // LANES
    row_stride = cap + SUBLANES
    n_be = n_batch * n_e
    idx3 = idx.reshape(n_be, 1, cap)
    gates3 = jnp.concatenate([gates.reshape(n_be, 1, cap), jnp.zeros((1, 1, cap), F32)], axis=0)

    def smem(index_map):
        return pl.BlockSpec((1, 1, cap), index_map, memory_space=pltpu.SMEM)

    hbm = pl.BlockSpec(memory_space=pl.ANY)
    c_ops, c_in, c_out, c_shapes = ([], [], [], []) if cast is None else _cast_plan(
        cast[0], cast[1], n_be, lambda b, e: b * n_e + e)
    return pl.pallas_call(
        functools.partial(_moe_kernel, row_stride, len(c_ops)),
        grid=(n_batch, n_e),
        in_specs=[
            smem(lambda b, e: (b * n_e + e, 0, 0)),
            smem(lambda b, e: (jnp.maximum(b * n_e + e - 1, 0), 0, 0)),
            smem(lambda b, e: (jnp.minimum(b * n_e + e + 1, n_be - 1), 0, 0)),
            smem(lambda b, e: (b * n_e + e, 0, 0)),
            smem(lambda b, e: (jnp.where(e == 0, n_be, b * n_e + e - 1), 0, 0)),
            hbm, hbm, hbm, hbm,
        ] + c_in,
        out_specs=[pl.BlockSpec((1, srows, LANES), lambda b, e: (b, 0, 0), pipeline_mode=pl.Buffered(1))]
        + c_out,
        out_shape=[jax.ShapeDtypeStruct((n_batch, srows, LANES), F32)] + c_shapes,
        scratch_shapes=[
            pltpu.VMEM((cap, d), BF16),
            pltpu.VMEM((rpt * row_stride, LANES), F32),
            pltpu.VMEM((rpt * row_stride, LANES), F32),
            pltpu.VMEM((srows, LANES), F32),
            pltpu.VMEM((2, d, fc), BF16),
            pltpu.VMEM((2, d, fc), BF16),
            pltpu.VMEM((2, fc, d), BF16),
            pltpu.SemaphoreType.DMA((2, 3)),
            pltpu.SemaphoreType.DMA(()),
        ],
        compiler_params=pltpu.CompilerParams(
            dimension_semantics=("arbitrary", "arbitrary"), vmem_limit_bytes=VMEM_LIMIT),
        name="moe",
    )(idx3, idx3, idx3, gates3, gates3, hs3, w_gate, w_up, w_down, *c_ops)


def _ln2_kernel(alpha, h_ref, f_ref, g_ref, b_ref, o_ref):
    tm, d = h_ref.shape
    o_ref[...] = _ln(alpha * h_ref[...] + _slab_rows(f_ref, tm, d // LANES), g_ref[...], b_ref[...])


def _ln2(h2d, ffn_slab, alpha, g, b):
    t, d = h2d.shape
    tm = LN_TM
    rpt = d // LANES
    return pl.pallas_call(
        functools.partial(_ln2_kernel, alpha),
        grid=(t // tm,),
        in_specs=[
            pl.BlockSpec((tm, d), lambda i: (i, 0)),
            pl.BlockSpec((tm * rpt, LANES), lambda i: (i, 0)),
            pl.BlockSpec((1, d), lambda i: (0, 0)),
            pl.BlockSpec((1, d), lambda i: (0, 0)),
        ],
        out_specs=pl.BlockSpec((tm, d), lambda i: (i, 0)),
        out_shape=jax.ShapeDtypeStruct((t, d), F32),
        compiler_params=pltpu.CompilerParams(
            dimension_semantics=("arbitrary",), vmem_limit_bytes=VMEM_LIMIT),
        name="ln2",
    )(h2d, ffn_slab, g, b)


def kernel(x, in_ln_g, in_ln_b, w_in, b_in, pool_w, pool_scale, sgu_ln_g, sgu_ln_b, sgu_w, sgu_b, p_a, p_b,
           w_out, ln1_g, ln1_b, w_router, w_gate, w_up, w_down, ln2_g, ln2_b):
    n_batch, seq, d = x.shape
    depth = w_in.shape[0]
    n_e = w_router.shape[-1]
    cap = CAPACITY_FACTOR * seq // n_e
    alpha = (2 * depth) ** 0.25
    t = n_batch * seq
    hd = sgu_ln_g.shape[-1] // SGU_HEADS
    rpt = d // LANES

    def row(a):
        return a.reshape(1, -1)

    gate_up = ((w_gate, 2), (w_up, 2))
    down = ((w_down, 1),)
    h = x.reshape(t, d)
    ffn = None
    experts = []
    pre_g, pre_b = in_ln_g, in_ln_b
    for l in range(depth):
        sgu_bias = jnp.repeat(sgu_b[l].T, hd, axis=1)
        h, hs, logits_t, *cast_out = _mixer(
            h, ffn, (gate_up + down if l == 0 else down, l), seq, alpha, row(pre_g), row(pre_b),
            w_in[l].astype(BF16), row(b_in[l]), pool_w[l].astype(BF16), row(pool_scale[l]),
            row(sgu_ln_g[l]), row(sgu_ln_b[l]), sgu_w[l].astype(BF16), sgu_bias,
            p_a[l].astype(BF16), p_b[l].astype(BF16), w_out[l].astype(BF16),
            row(ln1_g[l]), row(ln1_b[l]), w_router[l].T.astype(BF16))
        experts = experts + cast_out
        idx, gates = _route(logits_t, n_batch, seq, cap)
        ffn, *experts = _moe(hs.reshape(n_batch, seq * rpt, LANES), idx, gates, *experts,
                             (gate_up, l + 1) if l + 1 < depth else None)
        ffn = ffn.reshape(t * rpt, LANES)
        pre_g, pre_b = ln2_g[l], ln2_b[l]
    out = _ln2(h, ffn, alpha, row(pre_g), row(pre_b))
    return out.reshape(n_batch, seq, d)
```

```python
import functools

import jax
import jax.numpy as jnp
from jax import lax
from jax.experimental import pallas as pl
from jax.experimental.pallas import tpu as pltpu

F32 = jnp.float32
BF16 = jnp.bfloat16

POOL_WINDOWS = (2, 4, 8, 16)
POOL_HALO = 8
SGU_CHUNK = 128
SGU_HEADS = 8
CAPACITY_FACTOR = 2
LN_EPS = 1e-5
LANES = 128
SUBLANES = 8
VMEM_LIMIT = 249 * 256 * 1024
MIX_TM = 512
LN_TM = 1024


def _ln(x, g, b):
    mu = jnp.mean(x, axis=-1, keepdims=True)
    xc = x - mu
    var = jnp.mean(xc * xc, axis=-1, keepdims=True)
    return xc * lax.rsqrt(var + LN_EPS) * g + b


def _dot(a, b):
    return jnp.dot(a, b, preferred_element_type=F32)


def _slab_rows(ref, n, rpt):
    return jnp.concatenate([ref[pl.ds(q, n, stride=rpt), :] for q in range(rpt)], axis=1)


def _cast_plan(stacked, layer, n_steps, step_of):
    ops, in_specs, out_specs, out_shapes = [], [], [], []
    for w, parts in stacked:
        n_l, n_e, r, c = w.shape
        rows = n_e * r // n_steps
        assert 0 < rows <= r and r % rows == 0, "a cast block must not straddle two experts"
        per_expert = r // rows
        ops.append(w.reshape(n_l * n_steps, rows, c))

        def in_map(*ids, _l=layer):
            return (_l * n_steps + step_of(*ids), 0, 0)

        def out_map(*ids, _per=per_expert):
            lin = step_of(*ids)
            return (lin // _per, 0, lin % _per, 0)

        in_specs.append(pl.BlockSpec((1, rows, c), in_map))
        out_specs.append(pl.BlockSpec((1, parts, rows, c // parts), out_map))
        out_shapes.append(jax.ShapeDtypeStruct((n_e, parts, r, c // parts), BF16))
    return ops, in_specs, out_specs, out_shapes


def _cast_blocks(src_refs, dst_refs):
    for s, o in zip(src_refs, dst_refs):
        width = o.shape[3]
        for p in range(o.shape[1]):
            o[0, p] = s[0, :, p * width:(p + 1) * width].astype(BF16)


def _mixer_kernel(has_ffn, n_cast, seq, alpha, *refs):
    if has_ffn:
        x_ref, xp_ref, xn_ref, f_ref, fp_ref, fn_ref = refs[:6]
        refs = refs[6:]
    else:
        x_ref, xp_ref, xn_ref = refs[:3]
        refs = refs[3:]
    cast_src, refs = refs[:n_cast], refs[n_cast:]
    n_fixed = 16 + 3
    (preg_ref, preb_ref, win_ref, bin_ref, poolw_ref, pscale_ref, sg_ref, sb_ref, sw_ref, sbias_ref,
     pa_ref, pb_ref, wout_ref, l1g_ref, l1b_ref, wrt_ref,
     h_ref, hs_ref, logit_ref) = refs[:n_fixed]
    cast_dst, refs = refs[n_fixed:n_fixed + n_cast], refs[n_fixed + n_cast:]
    proj_ref, aext_ref, ya_ref, yb_ref = refs
    _cast_blocks(cast_src, cast_dst)
    tm, d = x_ref.shape
    rpt = d // LANES
    pw = ya_ref.shape[1]
    gd = pw // len(POOL_WINDOWS)
    sw = yb_ref.shape[1]
    hd = sw // SGU_HEADS
    o_u, o_v, o_ga, o_gb = pw, pw + sw, pw + 2 * sw, pw + 2 * sw + d

    i = pl.program_id(0)
    tiles_per_seq = seq // tm
    pos = i % tiles_per_seq

    x = x_ref[...]
    xp = xp_ref[...]
    xn = xn_ref[...]
    if has_ffn:
        x = alpha * x + _slab_rows(f_ref, tm, rpt)
        xp = alpha * xp + _slab_rows(fp_ref, POOL_HALO, rpt)
        xn = alpha * xn + _slab_rows(fn_ref, POOL_HALO, rpt)
    x = _ln(x, preg_ref[...], preb_ref[...])
    xp = _ln(xp, preg_ref[...], preb_ref[...])
    xn = _ln(xn, preg_ref[...], preb_ref[...])
    xb = x.astype(BF16)

    proj_ref[...] = _dot(xb, win_ref[...]) + bin_ref[...]
    ap = _dot(xp.astype(BF16), win_ref[:, 0:pw]) + bin_ref[:, 0:pw]
    an = _dot(xn.astype(BF16), win_ref[:, 0:pw]) + bin_ref[:, 0:pw]
    ap = jnp.where(pos == 0, 0.0, ap)
    an = jnp.where(pos == tiles_per_seq - 1, 0.0, an)
    aext_ref[0:POOL_HALO, :] = ap
    aext_ref[POOL_HALO:POOL_HALO + tm, :] = proj_ref[:, 0:pw]
    aext_ref[POOL_HALO + tm:2 * POOL_HALO + tm, :] = an

    for c in range(tm // SGU_CHUNK):
        r0 = c * SGU_CHUNK
        rows = pl.ds(r0, SGU_CHUNK)
        s = pos * tm + r0 + lax.broadcasted_iota(jnp.int32, (SGU_CHUNK, 1), 0)
        for g, w in enumerate(POOL_WINDOWS):
            cols = slice(g * gd, (g + 1) * gd)
            acc = aext_ref[pl.ds(POOL_HALO + r0 - w // 2, SGU_CHUNK), cols]
            for o in range(-w // 2 + 1, w // 2):
                acc = acc + aext_ref[pl.ds(POOL_HALO + r0 + o, SGU_CHUNK), cols]
            cnt = (jnp.minimum(s + w // 2, seq) - jnp.maximum(s - w // 2, 0)).astype(F32)
            pooled = acc / cnt - proj_ref[rows, cols]
            ya = _dot(pooled.astype(BF16), poolw_ref[g]) * pscale_ref[:, cols]
            ya_ref[rows, cols] = ya.astype(BF16)
        gu = jax.nn.gelu(proj_ref[rows, o_u:o_v])
        gv = jax.nn.gelu(proj_ref[rows, o_v:o_ga])
        vb = _ln(gv, sg_ref[...], sb_ref[...]).astype(BF16)
        for hh in range(SGU_HEADS):
            hc = slice(hh * hd, (hh + 1) * hd)
            mixed = _dot(sw_ref[hh], vb[:, hc]) + sbias_ref[:, hc]
            yb_ref[rows, hc] = (gu[:, hc] * mixed).astype(BF16)

    ta = _dot(ya_ref[...], pa_ref[...])
    tb = _dot(yb_ref[...], pb_ref[...])
    merged = (jax.nn.sigmoid(proj_ref[:, o_ga:o_gb]) * ta
              + jax.nn.sigmoid(proj_ref[:, o_gb:o_gb + d]) * tb)
    mix = _dot(merged.astype(BF16), wout_ref[...])
    h1 = _ln(alpha * x + mix, l1g_ref[...], l1b_ref[...])
    h_ref[...] = h1
    hb = h1.astype(BF16)
    logit_ref[...] = lax.dot_general(wrt_ref[...], hb, (((1,), (1,)), ((), ())), preferred_element_type=F32)

    for q in range(rpt):
        hs_ref[pl.ds(q, tm, stride=rpt), :] = h1[:, q * LANES:(q + 1) * LANES]


def _mixer(x2d, ffn_slab, cast, seq, alpha, pre_g, pre_b, w_in, b_in, pool_w, pool_scale, sgu_g, sgu_b, sgu_w,
           sgu_bias, p_a, p_b, w_out, ln1_g, ln1_b, w_rt):
    t, d = x2d.shape
    tm = MIX_TM
    rpt = d // LANES
    n_e = w_rt.shape[0]
    pw = p_a.shape[0]
    sw = p_b.shape[0]
    hb = tm // POOL_HALO
    n_hb = t // POOL_HALO
    has_ffn = ffn_slab is not None

    def full(a):
        nd = a.ndim
        return pl.BlockSpec(a.shape, lambda i, _n=nd: (0,) * _n)

    def prev_halo(i):
        return (jnp.maximum(i * hb - 1, 0), 0)

    def next_halo(i):
        return (jnp.minimum((i + 1) * hb, n_hb - 1), 0)

    acts = [x2d, x2d, x2d]
    act_specs = [
        pl.BlockSpec((tm, d), lambda i: (i, 0)),
        pl.BlockSpec((POOL_HALO, d), prev_halo),
        pl.BlockSpec((POOL_HALO, d), next_halo),
    ]
    if has_ffn:
        acts += [ffn_slab, ffn_slab, ffn_slab]
        act_specs += [
            pl.BlockSpec((tm * rpt, LANES), lambda i: (i, 0)),
            pl.BlockSpec((POOL_HALO * rpt, LANES), prev_halo),
            pl.BlockSpec((POOL_HALO * rpt, LANES), next_halo),
        ]
    weights = (pre_g, pre_b, w_in, b_in, pool_w, pool_scale, sgu_g, sgu_b, sgu_w, sgu_bias, p_a, p_b, w_out,
               ln1_g, ln1_b, w_rt)
    c_ops, c_in, c_out, c_shapes = ([], [], [], []) if cast is None else _cast_plan(
        cast[0], cast[1], t // tm, lambda i: i)
    return pl.pallas_call(
        functools.partial(_mixer_kernel, has_ffn, len(c_ops), seq, alpha),
        grid=(t // tm,),
        in_specs=act_specs + c_in + [full(a) for a in weights],
        out_specs=[
            pl.BlockSpec((tm, d), lambda i: (i, 0)),
            pl.BlockSpec((tm * rpt, LANES), lambda i: (i, 0)),
            pl.BlockSpec((n_e, tm), lambda i: (0, i)),
        ] + c_out,
        out_shape=[
            jax.ShapeDtypeStruct((t, d), F32),
            jax.ShapeDtypeStruct((t * rpt, LANES), F32),
            jax.ShapeDtypeStruct((n_e, t), F32),
        ] + c_shapes,
        scratch_shapes=[
            pltpu.VMEM((tm, w_in.shape[1]), F32),
            pltpu.VMEM((tm + 2 * POOL_HALO, pw), F32),
            pltpu.VMEM((tm, pw), BF16),
            pltpu.VMEM((tm, sw), BF16),
        ],
        compiler_params=pltpu.CompilerParams(
            dimension_semantics=("arbitrary",), vmem_limit_bytes=VMEM_LIMIT),
        name="mixer",
    )(*acts, *c_ops, *weights)


def _route_kernel(cap, logit_ref, idx_ref, gate_ref):
    n_e, seq = logit_ref.shape
    n_tiles = seq // LANES

    lg = logit_ref[...]
    ex = jnp.exp(lg - jnp.max(lg, axis=0, keepdims=True))
    aff = ex / jnp.sum(ex, axis=0, keepdims=True)

    def enough(cand):
        return jnp.sum(jnp.where(aff >= pltpu.bitcast(cand, F32), 1.0, 0.0), axis=1, keepdims=True) >= cap

    def bit_step(k, thr):
        hi = jnp.left_shift(jnp.int32(1), 29 - 2 * k)
        lo = jnp.left_shift(jnp.int32(1), 28 - 2 * k)
        return jnp.where(enough(thr | hi | lo), thr | hi | lo,
                         jnp.where(enough(thr | hi), thr | hi, jnp.where(enough(thr | lo), thr | lo, thr)))

    thr = lax.fori_loop(0, 15, bit_step, jnp.zeros((n_e, 1), jnp.int32))
    thr_f = pltpu.bitcast(thr, F32)
    gt = aff > thr_f
    eq = aff == thr_f
    need = cap - jnp.sum(jnp.where(gt, 1.0, 0.0), axis=1, keepdims=True)

    tri = (lax.broadcasted_iota(jnp.int32, (LANES, LANES), 0)
           <= lax.broadcasted_iota(jnp.int32, (LANES, LANES), 1)).astype(BF16)
    carry_gt = jnp.zeros((n_e, 1), F32)
    carry_eq = jnp.zeros((n_e, 1), F32)
    ranks = []
    for k in range(n_tiles):
        cols = slice(k * LANES, (k + 1) * LANES)
        gt_k = gt[:, cols]
        eq_k = eq[:, cols]
        c_gt = _dot(jnp.where(gt_k, 1.0, 0.0).astype(BF16), tri) + carry_gt
        c_eq = _dot(jnp.where(eq_k, 1.0, 0.0).astype(BF16), tri) + carry_eq
        carry_gt = c_gt[:, LANES - 1:LANES]
        carry_eq = c_eq[:, LANES - 1:LANES]
        sel_k = gt_k | (eq_k & (c_eq <= need))
        rank = c_gt + jnp.minimum(c_eq, need)
        ranks.append(jnp.where(sel_k, rank, 0.0))
    rank_all = jnp.concatenate(ranks, axis=1).astype(jnp.int32)

    pos_bits = seq.bit_length() - 1
    tok = lax.broadcasted_iota(jnp.int32, (n_e, seq), 1)
    key = jnp.where(rank_all > 0, tok | jnp.left_shift(tok - (rank_all - 1), pos_bits), 0)
    gate = aff
    for s in range(pos_bits):
        bit = jnp.int32(1 << (pos_bits + s))
        key_in = pltpu.roll(key, seq - (1 << s), 1)
        gate_in = pltpu.roll(gate, seq - (1 << s), 1)
        arriving = (key_in & bit) != 0
        leaving = (key & bit) != 0
        key = jnp.where(arriving, key_in, jnp.where(leaving, 0, key))
        gate = jnp.where(arriving, gate_in, gate)
    idx_ref[0] = key[:, :cap] & jnp.int32(seq - 1)
    gate_ref[0] = gate[:, :cap]


def _route(logits_t, n_batch, seq, cap):
    n_e = logits_t.shape[0]
    return pl.pallas_call(
        functools.partial(_route_kernel, cap),
        grid=(n_batch,),
        in_specs=[pl.BlockSpec((n_e, seq), lambda b: (0, b))],
        out_specs=[
            pl.BlockSpec((1, n_e, cap), lambda b: (b, 0, 0)),
            pl.BlockSpec((1, n_e, cap), lambda b: (b, 0, 0)),
        ],
        out_shape=[
            jax.ShapeDtypeStruct((n_batch, n_e, cap), jnp.int32),
            jax.ShapeDtypeStruct((n_batch, n_e, cap), F32),
        ],
        compiler_params=pltpu.CompilerParams(
            dimension_semantics=("arbitrary",), vmem_limit_bytes=VMEM_LIMIT),
        name="route",
    )(logits_t)


def _moe_kernel(row_stride, n_cast, residual_scale, idx_ref, idxp_ref, idxn_ref, gate_ref, gatep_ref, hs_hbm,
                wg_hbm, wu_hbm, wd_hbm, *refs):
    cast_src, refs = refs[:n_cast], refs[n_cast:]
    out_ref = refs[0]
    cast_dst = refs[1:1 + n_cast]
    xe_ref, xt_ref, yt_ref, hs_ref, wg_buf, wu_buf, wd_buf, sem, hs_sem = refs[1 + n_cast:]
    cap, d = xe_ref.shape
    fc = wg_buf.shape[2]
    rpt = d // LANES
    b = pl.program_id(0)
    e = pl.program_id(1)
    n_b = pl.num_programs(0)
    n_e = pl.num_programs(1)
    group = 8

    def weight_copies(expert, half):
        return (pltpu.make_async_copy(wg_hbm.at[expert, half], wg_buf.at[half], sem.at[half, 0]),
                pltpu.make_async_copy(wu_hbm.at[expert, half], wu_buf.at[half], sem.at[half, 1]),
                pltpu.make_async_copy(wd_hbm.at[expert, 0, pl.ds(half * fc, fc)], wd_buf.at[half],
                                      sem.at[half, 2]))

    def rows_copy(sequence):
        return pltpu.make_async_copy(hs_hbm.at[sequence], hs_ref, hs_sem)

    def gather_row(src_idx_ref, j):
        r = pl.multiple_of(src_idx_ref[0, 0, j] * rpt, rpt)
        xt_ref[pl.ds(j, rpt, stride=row_stride), :] = hs_ref[pl.ds(r, rpt), :]

    def scatter_rows(src_idx_ref, src_gate_ref, js):
        rows = [pl.multiple_of(src_idx_ref[0, 0, j] * rpt, rpt) for j in js]
        vals = [out_ref[0, pl.ds(r, rpt), :] + yt_ref[pl.ds(j, rpt, stride=row_stride), :] * src_gate_ref[0, 0, j]
                for r, j in zip(rows, js)]
        for r, v in zip(rows, vals):
            out_ref[0, pl.ds(r, rpt), :] = v

    def ffn_half(half):
        x = xe_ref[...]
        g = _dot(x, wg_buf[half])
        u = _dot(x, wu_buf[half])
        hid = (jax.nn.silu(g) * u).astype(BF16)
        return _dot(hid, wd_buf[half])

    first_step = (b == 0) & (e == 0)
    last_step = (b == n_b - 1) & (e == n_e - 1)

    @pl.when(first_step)
    def _():
        for c in weight_copies(e, 0):
            c.start()
        rows_copy(b).start()

    for c in weight_copies(e, 1):
        c.start()

    @pl.when((e == n_e - 1) & (b < n_b - 1))
    def _():
        rows_copy(b + 1).start()

    @pl.when(e == 0)
    def _():
        yt_ref[...] = jnp.zeros_like(yt_ref)
        if residual_scale is None:
            out_ref[...] = jnp.zeros_like(out_ref)
            rows_copy(b).wait()
        else:
            rows_copy(b).wait()
            out_ref[0] = residual_scale * hs_ref[...]

        def gather(jj, _):
            for u in range(group):
                gather_row(idx_ref, jj * group + u)
            return 0

        lax.fori_loop(0, cap // group, gather, 0)

    for c in weight_copies(e, 0):
        c.wait()

    _cast_blocks(cast_src, cast_dst)
    for q in range(rpt):
        xe_ref[:, q * LANES:(q + 1) * LANES] = xt_ref[pl.ds(q * row_stride, cap), :].astype(BF16)
    for j0 in range(0, cap, group):
        scatter_rows(idxp_ref, gatep_ref, range(j0, j0 + group))
    y = ffn_half(0)
    for q in range(rpt):
        yt_ref[pl.ds(q * row_stride, cap), :] = y[:, q * LANES:(q + 1) * LANES]

    @pl.when(jnp.logical_not(last_step))
    def _():
        for c in weight_copies(jnp.where(e + 1 < n_e, e + 1, 0), 0):
            c.start()

    for c in weight_copies(e, 1):
        c.wait()

    def second_half(gather_next):
        if gather_next:
            for j in range(cap):
                gather_row(idxn_ref, j)
        y = ffn_half(1)
        for q in range(rpt):
            yt_ref[pl.ds(q * row_stride, cap), :] += y[:, q * LANES:(q + 1) * LANES]

    @pl.when(e < n_e - 1)
    def _():
        second_half(True)

    @pl.when(e == n_e - 1)
    def _():
        second_half(False)

        def scatter(jj, _):
            scatter_rows(idx_ref, gate_ref, [jj * group + u for u in range(group)])
            return 0

        lax.fori_loop(0, cap // group, scatter, 0)


def _moe(hs3, idx, gates, w_gate, w_up, w_down, cast, residual_scale=None):
    n_batch, srows, _ = hs3.shape
    n_e, n_f, d, fc = w_gate.shape
    assert n_f == 2
    cap = idx.shape[-1]
    rpt = d // LANES
    row_stride = cap + SUBLANES
    n_be = n_batch * n_e
    idx3 = idx.reshape(n_be, 1, cap)
    gates3 = jnp.concatenate([gates.reshape(n_be, 1, cap), jnp.zeros((1, 1, cap), F32)], axis=0)

    def smem(index_map):
        return pl.BlockSpec((1, 1, cap), index_map, memory_space=pltpu.SMEM)

    hbm = pl.BlockSpec(memory_space=pl.ANY)
    c_ops, c_in, c_out, c_shapes = ([], [], [], []) if cast is None else _cast_plan(
        cast[0], cast[1], n_be, lambda b, e: b * n_e + e)
    return pl.pallas_call(
        functools.partial(_moe_kernel, row_stride, len(c_ops), residual_scale),
        grid=(n_batch, n_e),
        in_specs=[
            smem(lambda b, e: (b * n_e + e, 0, 0)),
            smem(lambda b, e: (jnp.maximum(b * n_e + e - 1, 0), 0, 0)),
            smem(lambda b, e: (jnp.minimum(b * n_e + e + 1, n_be - 1), 0, 0)),
            smem(lambda b, e: (b * n_e + e, 0, 0)),
            smem(lambda b, e: (jnp.where(e == 0, n_be, b * n_e + e - 1), 0, 0)),
            hbm, hbm, hbm, hbm,
        ] + c_in,
        out_specs=[pl.BlockSpec((1, srows, LANES), lambda b, e: (b, 0, 0), pipeline_mode=pl.Buffered(1))]
        + c_out,
        out_shape=[jax.ShapeDtypeStruct((n_batch, srows, LANES), F32)] + c_shapes,
        scratch_shapes=[
            pltpu.VMEM((cap, d), BF16),
            pltpu.VMEM((rpt * row_stride, LANES), F32),
            pltpu.VMEM((rpt * row_stride, LANES), F32),
            pltpu.VMEM((srows, LANES), F32),
            pltpu.VMEM((2, d, fc), BF16),
            pltpu.VMEM((2, d, fc), BF16),
            pltpu.VMEM((2, fc, d), BF16),
            pltpu.SemaphoreType.DMA((2, 3)),
            pltpu.SemaphoreType.DMA(()),
        ],
        compiler_params=pltpu.CompilerParams(
            dimension_semantics=("arbitrary", "arbitrary"), vmem_limit_bytes=VMEM_LIMIT),
        name="moe",
    )(idx3, idx3, idx3, gates3, gates3, hs3, w_gate, w_up, w_down, *c_ops)


def _ln2_kernel(s_ref, g_ref, b_ref, o_ref):
    tm, d = o_ref.shape
    o_ref[...] = _ln(_slab_rows(s_ref, tm, d // LANES), g_ref[...], b_ref[...])


def _ln2(sum_slab, t, d, g, b):
    tm = LN_TM
    rpt = d // LANES
    return pl.pallas_call(
        _ln2_kernel,
        grid=(t // tm,),
        in_specs=[
            pl.BlockSpec((tm * rpt, LANES), lambda i: (i, 0)),
            pl.BlockSpec((1, d), lambda i: (0, 0)),
            pl.BlockSpec((1, d), lambda i: (0, 0)),
        ],
        out_specs=pl.BlockSpec((tm, d), lambda i: (i, 0)),
        out_shape=jax.ShapeDtypeStruct((t, d), F32),
        compiler_params=pltpu.CompilerParams(
            dimension_semantics=("arbitrary",), vmem_limit_bytes=VMEM_LIMIT),
        name="ln2",
    )(sum_slab, g, b)


def kernel(x, in_ln_g, in_ln_b, w_in, b_in, pool_w, pool_scale, sgu_ln_g, sgu_ln_b, sgu_w, sgu_b, p_a, p_b,
           w_out, ln1_g, ln1_b, w_router, w_gate, w_up, w_down, ln2_g, ln2_b):
    n_batch, seq, d = x.shape
    depth = w_in.shape[0]
    n_e = w_router.shape[-1]
    cap = CAPACITY_FACTOR * seq // n_e
    alpha = (2 * depth) ** 0.25
    t = n_batch * seq
    hd = sgu_ln_g.shape[-1] // SGU_HEADS
    rpt = d // LANES

    def row(a):
        return a.reshape(1, -1)

    gate_up = ((w_gate, 2), (w_up, 2))
    down = ((w_down, 1),)
    h = x.reshape(t, d)
    ffn = None
    experts = []
    pre_g, pre_b = in_ln_g, in_ln_b
    for l in range(depth):
        sgu_bias = jnp.repeat(sgu_b[l].T, hd, axis=1)
        h, hs, logits_t, *cast_out = _mixer(
            h, ffn, (gate_up + down if l == 0 else down, l), seq, alpha, row(pre_g), row(pre_b),
            w_in[l].astype(BF16), row(b_in[l]), pool_w[l].astype(BF16), row(pool_scale[l]),
            row(sgu_ln_g[l]), row(sgu_ln_b[l]), sgu_w[l].astype(BF16), sgu_bias,
            p_a[l].astype(BF16), p_b[l].astype(BF16), w_out[l].astype(BF16),
            row(ln1_g[l]), row(ln1_b[l]), w_router[l].T.astype(BF16))
        experts = experts + cast_out
        idx, gates = _route(logits_t, n_batch, seq, cap)
        last = l + 1 == depth
        ffn, *experts = _moe(hs.reshape(n_batch, seq * rpt, LANES), idx, gates, *experts,
                             None if last else (gate_up, l + 1), alpha if last else None)
        ffn = ffn.reshape(t * rpt, LANES)
        pre_g, pre_b = ln2_g[l], ln2_b[l]
    out = _ln2(ffn, t, d, row(pre_g), row(pre_b))
    return out.reshape(n_batch, seq, d)
```

```python
import functools

import jax
import jax.numpy as jnp
from jax import lax
from jax.experimental import pallas as pl
from jax.experimental.pallas import tpu as pltpu

F32 = jnp.float32
BF16 = jnp.bfloat16

POOL_WINDOWS = (2, 4, 8, 16)
POOL_HALO = 8
SGU_CHUNK = 128
SGU_HEADS = 8
CAPACITY_FACTOR = 2
LN_EPS = 1e-5
LANES = 128
SUBLANES = 8
VMEM_LIMIT = 249 * 256 * 1024
MIX_TM = 512
LN_TM = 1024


def _ln(x, g, b):
    mu = jnp.mean(x, axis=-1, keepdims=True)
    xc = x - mu
    var = jnp.mean(xc * xc, axis=-1, keepdims=True)
    return xc * lax.rsqrt(var + LN_EPS) * g + b


def _dot(a, b):
    return jnp.dot(a, b, preferred_element_type=F32)


def _slab_rows(ref, n, rpt):
    return jnp.concatenate([ref[pl.ds(q, n, stride=rpt), :] for q in range(rpt)], axis=1)


def _cast_plan(stacked, layer, n_steps, step_of):
    ops, in_specs, out_specs, out_shapes = [], [], [], []
    for w, parts in stacked:
        n_l, n_e, r, c = w.shape
        rows = n_e * r // n_steps
        assert 0 < rows <= r and r % rows == 0, "a cast block must not straddle two experts"
        per_expert = r // rows
        ops.append(w.reshape(n_l * n_steps, rows, c))

        def in_map(*ids, _l=layer):
            return (_l * n_steps + step_of(*ids), 0, 0)

        def out_map(*ids, _per=per_expert):
            lin = step_of(*ids)
            return (lin // _per, 0, lin % _per, 0)

        in_specs.append(pl.BlockSpec((1, rows, c), in_map))
        out_specs.append(pl.BlockSpec((1, parts, rows, c // parts), out_map))
        out_shapes.append(jax.ShapeDtypeStruct((n_e, parts, r, c // parts), BF16))
    return ops, in_specs, out_specs, out_shapes


def _cast_blocks(src_refs, dst_refs):
    for s, o in zip(src_refs, dst_refs):
        width = o.shape[3]
        for p in range(o.shape[1]):
            o[0, p] = s[0, :, p * width:(p + 1) * width].astype(BF16)


def _mixer_kernel(has_ffn, n_cast, seq, alpha, *refs):
    if has_ffn:
        x_ref, xp_ref, xn_ref, f_ref, fp_ref, fn_ref = refs[:6]
        refs = refs[6:]
    else:
        x_ref, xp_ref, xn_ref = refs[:3]
        refs = refs[3:]
    cast_src, refs = refs[:n_cast], refs[n_cast:]
    n_fixed = 16 + 3
    (preg_ref, preb_ref, win_ref, bin_ref, poolw_ref, pscale_ref, sg_ref, sb_ref, sw_ref, sbias_ref,
     pa_ref, pb_ref, wout_ref, l1g_ref, l1b_ref, wrt_ref,
     h_ref, hs_ref, logit_ref) = refs[:n_fixed]
    cast_dst, refs = refs[n_fixed:n_fixed + n_cast], refs[n_fixed + n_cast:]
    proj_ref, aext_ref, ya_ref, yb_ref = refs
    _cast_blocks(cast_src, cast_dst)
    tm, d = x_ref.shape
    rpt = d // LANES
    pw = ya_ref.shape[1]
    gd = pw // len(POOL_WINDOWS)
    sw = yb_ref.shape[1]
    hd = sw // SGU_HEADS
    o_u, o_v, o_ga, o_gb = pw, pw + sw, pw + 2 * sw, pw + 2 * sw + d

    i = pl.program_id(0)
    tiles_per_seq = seq // tm
    pos = i % tiles_per_seq

    x = x_ref[...]
    xp = xp_ref[...]
    xn = xn_ref[...]
    if has_ffn:
        x = alpha * x + _slab_rows(f_ref, tm, rpt)
        xp = alpha * xp + _slab_rows(fp_ref, POOL_HALO, rpt)
        xn = alpha * xn + _slab_rows(fn_ref, POOL_HALO, rpt)
    x = _ln(x, preg_ref[...], preb_ref[...])
    xp = _ln(xp, preg_ref[...], preb_ref[...])
    xn = _ln(xn, preg_ref[...], preb_ref[...])
    xb = x.astype(BF16)

    proj_ref[...] = _dot(xb, win_ref[...]) + bin_ref[...]
    ap = _dot(xp.astype(BF16), win_ref[:, 0:pw]) + bin_ref[:, 0:pw]
    an = _dot(xn.astype(BF16), win_ref[:, 0:pw]) + bin_ref[:, 0:pw]
    ap = jnp.where(pos == 0, 0.0, ap)
    an = jnp.where(pos == tiles_per_seq - 1, 0.0, an)
    aext_ref[0:POOL_HALO, :] = ap
    aext_ref[POOL_HALO:POOL_HALO + tm, :] = proj_ref[:, 0:pw]
    aext_ref[POOL_HALO + tm:2 * POOL_HALO + tm, :] = an

    for c in range(tm // SGU_CHUNK):
        r0 = c * SGU_CHUNK
        rows = pl.ds(r0, SGU_CHUNK)
        s = pos * tm + r0 + lax.broadcasted_iota(jnp.int32, (SGU_CHUNK, 1), 0)
        for g, w in enumerate(POOL_WINDOWS):
            cols = slice(g * gd, (g + 1) * gd)
            acc = aext_ref[pl.ds(POOL_HALO + r0 - w // 2, SGU_CHUNK), cols]
            for o in range(-w // 2 + 1, w // 2):
                acc = acc + aext_ref[pl.ds(POOL_HALO + r0 + o, SGU_CHUNK), cols]
            cnt = (jnp.minimum(s + w // 2, seq) - jnp.maximum(s - w // 2, 0)).astype(F32)
            pooled = acc / cnt - proj_ref[rows, cols]
            ya = _dot(pooled.astype(BF16), poolw_ref[g]) * pscale_ref[:, cols]
            ya_ref[rows, cols] = ya.astype(BF16)
        gu = jax.nn.gelu(proj_ref[rows, o_u:o_v])
        gv = jax.nn.gelu(proj_ref[rows, o_v:o_ga])
        vb = _ln(gv, sg_ref[...], sb_ref[...]).astype(BF16)
        for hh in range(SGU_HEADS):
            hc = slice(hh * hd, (hh + 1) * hd)
            mixed = _dot(sw_ref[hh], vb[:, hc]) + sbias_ref[:, hc]
            yb_ref[rows, hc] = (gu[:, hc] * mixed).astype(BF16)

    ta = _dot(ya_ref[...], pa_ref[...])
    tb = _dot(yb_ref[...], pb_ref[...])
    merged = (jax.nn.sigmoid(proj_ref[:, o_ga:o_gb]) * ta
              + jax.nn.sigmoid(proj_ref[:, o_gb:o_gb + d]) * tb)
    mix = _dot(merged.astype(BF16), wout_ref[...])
    h1 = _ln(alpha * x + mix, l1g_ref[...], l1b_ref[...])
    h_ref[...] = h1
    hb = h1.astype(BF16)
    logit_ref[...] = lax.dot_general(wrt_ref[...], hb, (((1,), (1,)), ((), ())), preferred_element_type=F32)

    for q in range(rpt):
        hs_ref[pl.ds(q, tm, stride=rpt), :] = h1[:, q * LANES:(q + 1) * LANES]


def _mixer(x2d, ffn_slab, cast, seq, alpha, pre_g, pre_b, w_in, b_in, pool_w, pool_scale, sgu_g, sgu_b, sgu_w,
           sgu_bias, p_a, p_b, w_out, ln1_g, ln1_b, w_rt):
    t, d = x2d.shape
    tm = MIX_TM
    rpt = d // LANES
    n_e = w_rt.shape[0]
    pw = p_a.shape[0]
    sw = p_b.shape[0]
    hb = tm // POOL_HALO
    n_hb = t // POOL_HALO
    has_ffn = ffn_slab is not None

    def full(a):
        nd = a.ndim
        return pl.BlockSpec(a.shape, lambda i, _n=nd: (0,) * _n)

    def prev_halo(i):
        return (jnp.maximum(i * hb - 1, 0), 0)

    def next_halo(i):
        return (jnp.minimum((i + 1) * hb, n_hb - 1), 0)

    acts = [x2d, x2d, x2d]
    act_specs = [
        pl.BlockSpec((tm, d), lambda i: (i, 0)),
        pl.BlockSpec((POOL_HALO, d), prev_halo),
        pl.BlockSpec((POOL_HALO, d), next_halo),
    ]
    if has_ffn:
        acts += [ffn_slab, ffn_slab, ffn_slab]
        act_specs += [
            pl.BlockSpec((tm * rpt, LANES), lambda i: (i, 0)),
            pl.BlockSpec((POOL_HALO * rpt, LANES), prev_halo),
            pl.BlockSpec((POOL_HALO * rpt, LANES), next_halo),
        ]
    weights = (pre_g, pre_b, w_in, b_in, pool_w, pool_scale, sgu_g, sgu_b, sgu_w, sgu_bias, p_a, p_b, w_out,
               ln1_g, ln1_b, w_rt)
    c_ops, c_in, c_out, c_shapes = ([], [], [], []) if cast is None else _cast_plan(
        cast[0], cast[1], t // tm, lambda i: i)
    return pl.pallas_call(
        functools.partial(_mixer_kernel, has_ffn, len(c_ops), seq, alpha),
        grid=(t // tm,),
        in_specs=act_specs + c_in + [full(a) for a in weights],
        out_specs=[
            pl.BlockSpec((tm, d), lambda i: (i, 0)),
            pl.BlockSpec((tm * rpt, LANES), lambda i: (i, 0)),
            pl.BlockSpec((n_e, tm), lambda i: (0, i)),
        ] + c_out,
        out_shape=[
            jax.ShapeDtypeStruct((t, d), F32),
            jax.ShapeDtypeStruct((t * rpt, LANES), F32),
            jax.ShapeDtypeStruct((n_e, t), F32),
        ] + c_shapes,
        scratch_shapes=[
            pltpu.VMEM((tm, w_in.shape[1]), F32),
            pltpu.VMEM((tm + 2 * POOL_HALO, pw), F32),
            pltpu.VMEM((tm, pw), BF16),
            pltpu.VMEM((tm, sw), BF16),
        ],
        compiler_params=pltpu.CompilerParams(
            dimension_semantics=("arbitrary",), vmem_limit_bytes=VMEM_LIMIT),
        name="mixer",
    )(*acts, *c_ops, *weights)


def _route_kernel(cap, logit_ref, idx_ref, gate_ref):
    n_e, seq = logit_ref.shape
    n_tiles = seq // LANES

    lg = logit_ref[...]
    ex = jnp.exp(lg - jnp.max(lg, axis=0, keepdims=True))
    aff = ex / jnp.sum(ex, axis=0, keepdims=True)

    def enough(cand):
        return jnp.sum(jnp.where(aff >= pltpu.bitcast(cand, F32), 1.0, 0.0), axis=1, keepdims=True) >= cap

    def bit_step(k, thr):
        hi = jnp.left_shift(jnp.int32(1), 29 - 2 * k)
        lo = jnp.left_shift(jnp.int32(1), 28 - 2 * k)
        return jnp.where(enough(thr | hi | lo), thr | hi | lo,
                         jnp.where(enough(thr | hi), thr | hi, jnp.where(enough(thr | lo), thr | lo, thr)))

    thr = lax.fori_loop(0, 15, bit_step, jnp.zeros((n_e, 1), jnp.int32))
    thr_f = pltpu.bitcast(thr, F32)
    gt = aff > thr_f
    eq = aff == thr_f
    need = cap - jnp.sum(jnp.where(gt, 1.0, 0.0), axis=1, keepdims=True)

    tri = (lax.broadcasted_iota(jnp.int32, (LANES, LANES), 0)
           <= lax.broadcasted_iota(jnp.int32, (LANES, LANES), 1)).astype(BF16)
    carry_gt = jnp.zeros((n_e, 1), F32)
    carry_eq = jnp.zeros((n_e, 1), F32)
    ranks = []
    for k in range(n_tiles):
        cols = slice(k * LANES, (k + 1) * LANES)
        gt_k = gt[:, cols]
        eq_k = eq[:, cols]
        c_gt = _dot(jnp.where(gt_k, 1.0, 0.0).astype(BF16), tri) + carry_gt
        c_eq = _dot(jnp.where(eq_k, 1.0, 0.0).astype(BF16), tri) + carry_eq
        carry_gt = c_gt[:, LANES - 1:LANES]
        carry_eq = c_eq[:, LANES - 1:LANES]
        sel_k = gt_k | (eq_k & (c_eq <= need))
        rank = c_gt + jnp.minimum(c_eq, need)
        ranks.append(jnp.where(sel_k, rank, 0.0))
    rank_all = jnp.concatenate(ranks, axis=1).astype(jnp.int32)

    pos_bits = seq.bit_length() - 1
    tok = lax.broadcasted_iota(jnp.int32, (n_e, seq), 1)
    key = jnp.where(rank_all > 0, tok | jnp.left_shift(tok - (rank_all - 1), pos_bits), 0)
    gate = aff
    for s in range(pos_bits):
        bit = jnp.int32(1 << (pos_bits + s))
        key_in = pltpu.roll(key, seq - (1 << s), 1)
        gate_in = pltpu.roll(gate, seq - (1 << s), 1)
        arriving = (key_in & bit) != 0
        leaving = (key & bit) != 0
        key = jnp.where(arriving, key_in, jnp.where(leaving, 0, key))
        gate = jnp.where(arriving, gate_in, gate)
    idx_ref[0] = key[:, :cap] & jnp.int32(seq - 1)
    gate_ref[0] = gate[:, :cap]


def _route(logits_t, n_batch, seq, cap):
    n_e = logits_t.shape[0]
    return pl.pallas_call(
        functools.partial(_route_kernel, cap),
        grid=(n_batch,),
        in_specs=[pl.BlockSpec((n_e, seq), lambda b: (0, b))],
        out_specs=[
            pl.BlockSpec((1, n_e, cap), lambda b: (b, 0, 0)),
            pl.BlockSpec((1, n_e, cap), lambda b: (b, 0, 0)),
        ],
        out_shape=[
            jax.ShapeDtypeStruct((n_batch, n_e, cap), jnp.int32),
            jax.ShapeDtypeStruct((n_batch, n_e, cap), F32),
        ],
        compiler_params=pltpu.CompilerParams(
            dimension_semantics=("arbitrary",), vmem_limit_bytes=VMEM_LIMIT),
        name="route",
    )(logits_t)


def _moe_kernel(row_stride, n_cast, residual_scale, idx_ref, idxp_ref, idxn_ref, gate_ref, gatep_ref, hs_hbm,
                wg_hbm, wu_hbm, wd_hbm, *refs):
    cast_src, refs = refs[:n_cast], refs[n_cast:]
    out_ref = refs[0]
    cast_dst = refs[1:1 + n_cast]
    xe_ref, xt_ref, yt_ref, hs_ref, wg_buf, wu_buf, wd_buf, sem, hs_sem = refs[1 + n_cast:]
    cap, d = xe_ref.shape
    fc = wg_buf.shape[2]
    rpt = d // LANES
    b = pl.program_id(0)
    e = pl.program_id(1)
    n_b = pl.num_programs(0)
    n_e = pl.num_programs(1)
    group = 8

    def weight_copies(expert, half):
        return (pltpu.make_async_copy(wg_hbm.at[expert, half], wg_buf.at[half], sem.at[half, 0]),
                pltpu.make_async_copy(wu_hbm.at[expert, half], wu_buf.at[half], sem.at[half, 1]),
                pltpu.make_async_copy(wd_hbm.at[expert, 0, pl.ds(half * fc, fc)], wd_buf.at[half],
                                      sem.at[half, 2]))

    def rows_copy(sequence):
        return pltpu.make_async_copy(hs_hbm.at[sequence], hs_ref, hs_sem)

    def gather_row(src_idx_ref, j):
        r = pl.multiple_of(src_idx_ref[0, 0, j] * rpt, rpt)
        xt_ref[pl.ds(j, rpt, stride=row_stride), :] = hs_ref[pl.ds(r, rpt), :]

    def scatter_rows(src_idx_ref, src_gate_ref, js):
        rows = [pl.multiple_of(src_idx_ref[0, 0, j] * rpt, rpt) for j in js]
        vals = [out_ref[0, pl.ds(r, rpt), :] + yt_ref[pl.ds(j, rpt, stride=row_stride), :] * src_gate_ref[0, 0, j]
                for r, j in zip(rows, js)]
        for r, v in zip(rows, vals):
            out_ref[0, pl.ds(r, rpt), :] = v

    def ffn_half(half):
        x = xe_ref[...]
        g = _dot(x, wg_buf[half])
        u = _dot(x, wu_buf[half])
        hid = (jax.nn.silu(g) * u).astype(BF16)
        return _dot(hid, wd_buf[half])

    first_step = (b == 0) & (e == 0)
    last_step = (b == n_b - 1) & (e == n_e - 1)

    @pl.when(first_step)
    def _():
        for c in weight_copies(e, 0):
            c.start()
        rows_copy(b).start()

    for c in weight_copies(e, 1):
        c.start()

    @pl.when((e == n_e - 1) & (b < n_b - 1))
    def _():
        rows_copy(b + 1).start(priority=1)

    @pl.when(e == 0)
    def _():
        yt_ref[...] = jnp.zeros_like(yt_ref)
        if residual_scale is None:
            out_ref[...] = jnp.zeros_like(out_ref)
            rows_copy(b).wait()
        else:
            rows_copy(b).wait()
            out_ref[0] = residual_scale * hs_ref[...]

        def gather(jj, _):
            for u in range(group):
                gather_row(idx_ref, jj * group + u)
            return 0

        lax.fori_loop(0, cap // group, gather, 0)

    for c in weight_copies(e, 0):
        c.wait()

    _cast_blocks(cast_src, cast_dst)
    for q in range(rpt):
        xe_ref[:, q * LANES:(q + 1) * LANES] = xt_ref[pl.ds(q * row_stride, cap), :].astype(BF16)
    for j0 in range(0, cap, group):
        scatter_rows(idxp_ref, gatep_ref, range(j0, j0 + group))
    y = ffn_half(0)
    for q in range(rpt):
        yt_ref[pl.ds(q * row_stride, cap), :] = y[:, q * LANES:(q + 1) * LANES]

    @pl.when(jnp.logical_not(last_step))
    def _():
        for c in weight_copies(jnp.where(e + 1 < n_e, e + 1, 0), 0):
            c.start()

    for c in weight_copies(e, 1):
        c.wait()

    def second_half(gather_next):
        if gather_next:
            for j in range(cap):
                gather_row(idxn_ref, j)
        y = ffn_half(1)
        for q in range(rpt):
            yt_ref[pl.ds(q * row_stride, cap), :] += y[:, q * LANES:(q + 1) * LANES]

    @pl.when(e < n_e - 1)
    def _():
        second_half(True)

    @pl.when(e == n_e - 1)
    def _():
        second_half(False)

        def scatter(jj, _):
            scatter_rows(idx_ref, gate_ref, [jj * group + u for u in range(group)])
            return 0

        lax.fori_loop(0, cap // group, scatter, 0)


def _moe(hs3, idx, gates, w_gate, w_up, w_down, cast, residual_scale=None):
    n_batch, srows, _ = hs3.shape
    n_e, n_f, d, fc = w_gate.shape
    assert n_f == 2
    cap = idx.shape[-1]
    rpt = d // LANES
    row_stride = cap + SUBLANES
    n_be = n_batch * n_e
    idx3 = idx.reshape(n_be, 1, cap)
    gates3 = jnp.concatenate([gates.reshape(n_be, 1, cap), jnp.zeros((1, 1, cap), F32)], axis=0)

    def smem(index_map):
        return pl.BlockSpec((1, 1, cap), index_map, memory_space=pltpu.SMEM)

    hbm = pl.BlockSpec(memory_space=pl.ANY)
    c_ops, c_in, c_out, c_shapes = ([], [], [], []) if cast is None else _cast_plan(
        cast[0], cast[1], n_be, lambda b, e: b * n_e + e)
    return pl.pallas_call(
        functools.partial(_moe_kernel, row_stride, len(c_ops), residual_scale),
        grid=(n_batch, n_e),
        in_specs=[
            smem(lambda b, e: (b * n_e + e, 0, 0)),
            smem(lambda b, e: (jnp.maximum(b * n_e + e - 1, 0), 0, 0)),
            smem(lambda b, e: (jnp.minimum(b * n_e + e + 1, n_be - 1), 0, 0)),
            smem(lambda b, e: (b * n_e + e, 0, 0)),
            smem(lambda b, e: (jnp.where(e == 0, n_be, b * n_e + e - 1), 0, 0)),
            hbm, hbm, hbm, hbm,
        ] + c_in,
        out_specs=[pl.BlockSpec((1, srows, LANES), lambda b, e: (b, 0, 0), pipeline_mode=pl.Buffered(1))]
        + c_out,
        out_shape=[jax.ShapeDtypeStruct((n_batch, srows, LANES), F32)] + c_shapes,
        scratch_shapes=[
            pltpu.VMEM((cap, d), BF16),
            pltpu.VMEM((rpt * row_stride, LANES), F32),
            pltpu.VMEM((rpt * row_stride, LANES), F32),
            pltpu.VMEM((srows, LANES), F32),
            pltpu.VMEM((2, d, fc), BF16),
            pltpu.VMEM((2, d, fc), BF16),
            pltpu.VMEM((2, fc, d), BF16),
            pltpu.SemaphoreType.DMA((2, 3)),
            pltpu.SemaphoreType.DMA(()),
        ],
        compiler_params=pltpu.CompilerParams(
            dimension_semantics=("arbitrary", "arbitrary"), vmem_limit_bytes=VMEM_LIMIT),
        name="moe",
    )(idx3, idx3, idx3, gates3, gates3, hs3, w_gate, w_up, w_down, *c_ops)


def _ln2_kernel(s_ref, g_ref, b_ref, o_ref):
    tm, d = o_ref.shape
    o_ref[...] = _ln(_slab_rows(s_ref, tm, d // LANES), g_ref[...], b_ref[...])


def _ln2(sum_slab, t, d, g, b):
    tm = LN_TM
    rpt = d // LANES
    return pl.pallas_call(
        _ln2_kernel,
        grid=(t // tm,),
        in_specs=[
            pl.BlockSpec((tm * rpt, LANES), lambda i: (i, 0)),
            pl.BlockSpec((1, d), lambda i: (0, 0)),
            pl.BlockSpec((1, d), lambda i: (0, 0)),
        ],
        out_specs=pl.BlockSpec((tm, d), lambda i: (i, 0)),
        out_shape=jax.ShapeDtypeStruct((t, d), F32),
        compiler_params=pltpu.CompilerParams(
            dimension_semantics=("arbitrary",), vmem_limit_bytes=VMEM_LIMIT),
        name="ln2",
    )(sum_slab, g, b)


def kernel(x, in_ln_g, in_ln_b, w_in, b_in, pool_w, pool_scale, sgu_ln_g, sgu_ln_b, sgu_w, sgu_b, p_a, p_b,
           w_out, ln1_g, ln1_b, w_router, w_gate, w_up, w_down, ln2_g, ln2_b):
    n_batch, seq, d = x.shape
    depth = w_in.shape[0]
    n_e = w_router.shape[-1]
    cap = CAPACITY_FACTOR * seq // n_e
    alpha = (2 * depth) ** 0.25
    t = n_batch * seq
    hd = sgu_ln_g.shape[-1] // SGU_HEADS
    rpt = d // LANES

    def row(a):
        return a.reshape(1, -1)

    gate_up = ((w_gate, 2), (w_up, 2))
    down = ((w_down, 1),)
    h = x.reshape(t, d)
    ffn = None
    experts = []
    pre_g, pre_b = in_ln_g, in_ln_b
    for l in range(depth):
        sgu_bias = jnp.repeat(sgu_b[l].T, hd, axis=1)
        h, hs, logits_t, *cast_out = _mixer(
            h, ffn, (gate_up + down if l == 0 else down, l), seq, alpha, row(pre_g), row(pre_b),
            w_in[l].astype(BF16), row(b_in[l]), pool_w[l].astype(BF16), row(pool_scale[l]),
            row(sgu_ln_g[l]), row(sgu_ln_b[l]), sgu_w[l].astype(BF16), sgu_bias,
            p_a[l].astype(BF16), p_b[l].astype(BF16), w_out[l].astype(BF16),
            row(ln1_g[l]), row(ln1_b[l]), w_router[l].T.astype(BF16))
        experts = experts + cast_out
        idx, gates = _route(logits_t, n_batch, seq, cap)
        last = l + 1 == depth
        ffn, *experts = _moe(hs.reshape(n_batch, seq * rpt, LANES), idx, gates, *experts,
                             None if last else (gate_up, l + 1), alpha if last else None)
        ffn = ffn.reshape(t * rpt, LANES)
        pre_g, pre_b = ln2_g[l], ln2_b[l]
    out = _ln2(ffn, t, d, row(pre_g), row(pre_b))
    return out.reshape(n_batch, seq, d)
```
